```python
import jax, jax.numpy as jnp
from jax import lax
import numpy as np

D_MODEL = 1024
BATCH = 8
SEQ = 4096
DEPTH = 1

CHUNK = 64
D_MIX = D_MODEL
D_RWKV = D_MIX // 2
RWKV_HEAD = 64
RWKV_HEADS = D_RWKV // RWKV_HEAD
D_MLSTM = D_MIX - D_RWKV
MLSTM_HEADS = 4
MLSTM_HEAD = D_MLSTM // MLSTM_HEADS
DECAY_LORA = 64
ICL_LORA = 64
CONV_K = 4
NORM_EPS = 1e-6
RWKV_GN_EPS = 64e-5
MLSTM_LN_EPS = 1e-6

RWKV_SHIFT_SPLITS = (D_RWKV, D_RWKV, D_RWKV, DECAY_LORA, ICL_LORA)
D_RWKV_SHIFT = sum(RWKV_SHIFT_SPLITS)
REST_SPLITS = (D_RWKV,
               D_MLSTM, D_MLSTM, D_MLSTM, D_MLSTM,
               MLSTM_HEADS, MLSTM_HEADS,
               D_MLSTM)
D_IN = D_RWKV_SHIFT + sum(REST_SPLITS)

kernel_name = 'hymba_rwkv7_mlstm_adaln_block'


def _offsets(sizes):
    return [int(s) for s in np.cumsum(sizes)[:-1]]


def _rmsnorm(x, gain):
    xf = x.astype(jnp.float32)
    return xf * lax.rsqrt(jnp.mean(xf * xf, axis=-1, keepdims=True) + NORM_EPS) * gain


def _token_shift_lerp(u, mu):
    prev = jnp.pad(u, ((0, 0), (1, 0), (0, 0)))[:, :-1]
    return u + mu * (prev - u)


def _causal_dwconv(u, w, bias):
    K = w.shape[0]
    T = u.shape[1]
    up = jnp.pad(u, ((0, 0), (K - 1, 0), (0, 0)))
    out = bias
    for j in range(K):
        out = out + w[j] * up[:, j:j + T]
    return out


def _rwkv7_step(S, inp):
    r, w, k, v, kk, a = inp
    sa = jnp.einsum('bhvk,bhk->bhv', S, -kk)
    S = S * w[:, :, None, :] + sa[..., None] * (kk * a)[:, :, None, :] + v[..., None] * k[:, :, None, :]
    y = jnp.einsum('bhvk,bhk->bhv', S, r)
    return S, y


def _rwkv7_branch(u, z, mu, w_decay_up, w_decay0, w_icl_up, a0, k_k, k_a, r_k, gn_w, gn_b):
    B, T, _ = u.shape
    u = _token_shift_lerp(u.astype(jnp.float32), mu)
    r, k, v, wd, ad = jnp.split(u, _offsets(RWKV_SHIFT_SPLITS), axis=-1)
    w_log = -jax.nn.softplus(-(w_decay0 + jnp.tanh(wd) @ w_decay_up)) - 0.5
    decay = jnp.exp(-jnp.exp(w_log))
    a = jax.nn.sigmoid(a0 + ad @ w_icl_up)
    hd = lambda t: t.reshape(B, T, RWKV_HEADS, RWKV_HEAD)
    kk = hd(k * k_k)
    kk = kk / jnp.maximum(jnp.sqrt(jnp.sum(kk * kk, axis=-1, keepdims=True)), 1e-12)
    k = k * (1.0 + (a - 1.0) * k_a)
    r_h, w_h, k_h, v_h, a_h = hd(r), hd(decay), hd(k), hd(v), hd(a)
    xs = tuple(jnp.swapaxes(t, 0, 1) for t in (r_h, w_h, k_h, v_h, kk, a_h))
    S0 = jnp.zeros((B, RWKV_HEADS, RWKV_HEAD, RWKV_HEAD), jnp.float32)
    _, y = lax.scan(_rwkv7_step, S0, xs)
    y = jnp.swapaxes(y, 0, 1)
    mean = jnp.mean(y, axis=-1, keepdims=True)
    var = jnp.mean(jnp.square(y - mean), axis=-1, keepdims=True)
    y = ((y - mean) * lax.rsqrt(var + RWKV_GN_EPS)).reshape(B, T, D_RWKV) * gn_w + gn_b
    bonus = jnp.sum(r_h * k_h * r_k.reshape(RWKV_HEADS, RWKV_HEAD), axis=-1, keepdims=True) * v_h
    y = y + bonus.reshape(B, T, D_RWKV)
    return y * jax.nn.silu(z.astype(jnp.float32))


def _mlstm_chunk_step(carry, inp):
    C, n, m = carry
    q, k, v, ig, lf = inp
    L = q.shape[2]
    b = jnp.cumsum(lf, axis=-1)
    b_last = b[..., -1]
    causal = jnp.tril(jnp.ones((L, L), dtype=bool))
    d_log = jnp.where(causal, b[..., :, None] - b[..., None, :] + ig[..., None, :], -jnp.inf)
    inter_log = b + m[..., None]
    m_t = jnp.maximum(inter_log, jnp.max(d_log, axis=-1))
    d_w = jnp.exp(d_log - m_t[..., None])
    scores = jnp.einsum('bhtd,bhsd->bhts', q, k) * d_w
    inter = jnp.exp(inter_log - m_t)
    num = jnp.einsum('bhts,bhsd->bhtd', scores, v) + inter[..., None] * jnp.einsum('bhvk,bhtk->bhtv', C, q)
    den = jnp.sum(scores, axis=-1) + inter * jnp.einsum('bhk,bhtk->bht', n, q)
    h = num / jnp.maximum(jnp.abs(den), jnp.exp(-m_t))[..., None]
    g = b_last[..., None] - b + ig
    m_new = jnp.maximum(b_last + m, jnp.max(g, axis=-1))
    w_s = jnp.exp(g - m_new[..., None])
    carry_scale = jnp.exp(b_last + m - m_new)
    C = carry_scale[..., None, None] * C + jnp.einsum('bhs,bhsv,bhsk->bhvk', w_s, v, k)
    n = carry_scale[..., None] * n + jnp.einsum('bhs,bhsk->bhk', w_s, k)
    return (C, n, m_new), h


def _mlstm_branch(q, k, v, o, ig, fg, z, conv_w, conv_b, b_i, b_f, ln_w, skip):
    B, T, _ = q.shape
    qk = jax.nn.silu(_causal_dwconv(jnp.concatenate([q, k], axis=-1).astype(jnp.float32), conv_w, conv_b))
    q_c, k_c = jnp.split(qk, 2, axis=-1)
    nc = T // CHUNK
    def chunked(t):
        return t.reshape(B, nc, CHUNK, MLSTM_HEADS, MLSTM_HEAD).transpose(1, 0, 3, 2, 4)
    def chunked_gate(t):
        return t.reshape(B, nc, CHUNK, MLSTM_HEADS).transpose(1, 0, 3, 2)
    i_pre = ig.astype(jnp.float32) + b_i
    log_f = jax.nn.log_sigmoid(fg.astype(jnp.float32) + b_f)
    xs = (chunked(q_c), chunked(k_c) * (MLSTM_HEAD ** -0.5), chunked(v.astype(jnp.float32)),
          chunked_gate(i_pre), chunked_gate(log_f))
    carry0 = (jnp.zeros((B, MLSTM_HEADS, MLSTM_HEAD, MLSTM_HEAD), jnp.float32),
              jnp.zeros((B, MLSTM_HEADS, MLSTM_HEAD), jnp.float32),
              jnp.zeros((B, MLSTM_HEADS), jnp.float32))
    _, h = lax.scan(_mlstm_chunk_step, carry0, xs)
    h = h.transpose(1, 0, 3, 2, 4).reshape(B, T, MLSTM_HEADS, MLSTM_HEAD)
    mean = jnp.mean(h, axis=-1, keepdims=True)
    var = jnp.mean(jnp.square(h - mean), axis=-1, keepdims=True)
    h = ((h - mean) * lax.rsqrt(var + MLSTM_LN_EPS)).reshape(B, T, D_MLSTM) * ln_w
    h = jax.nn.sigmoid(o.astype(jnp.float32)) * h + skip * q_c
    return h * jax.nn.silu(z.astype(jnp.float32))


def setup_inputs(seed: int = 0) -> dict:
    key = jax.random.key(seed)
    ks = jax.random.split(key, 32)
    def nrm(k, shape, scale):
        return jax.random.normal(k, shape, jnp.float32) * scale
    x = nrm(ks[0], (BATCH, SEQ, D_MODEL), 1.0)
    c = nrm(ks[1], (BATCH, D_MODEL), 1.0)
    w_ada = nrm(ks[2], (DEPTH, D_MODEL, 3 * D_MODEL), D_MODEL ** -0.5)
    b_ada = nrm(ks[3], (DEPTH, 3 * D_MODEL), 0.02)
    norm_gain = 1.0 + nrm(ks[4], (DEPTH, D_MODEL), 0.02)
    w_in = nrm(ks[5], (DEPTH, D_MODEL, D_IN), D_MODEL ** -0.5)
    mu_rwkv = jax.random.uniform(ks[6], (DEPTH, D_RWKV_SHIFT), jnp.float32)
    w_decay_up = nrm(ks[7], (DEPTH, DECAY_LORA, D_RWKV), 0.5 * DECAY_LORA ** -0.5)
    w_decay0 = jnp.linspace(-6.5, -1.5, D_RWKV, dtype=jnp.float32)[None] + nrm(ks[8], (DEPTH, D_RWKV), 0.1)
    w_icl_up = nrm(ks[9], (DEPTH, ICL_LORA, D_RWKV), ICL_LORA ** -0.5)
    a0 = nrm(ks[10], (DEPTH, D_RWKV), 0.1)
    k_k = 0.85 + nrm(ks[11], (DEPTH, D_RWKV), 0.02)
    k_a = 1.0 + nrm(ks[12], (DEPTH, D_RWKV), 0.02)
    r_k = nrm(ks[13], (DEPTH, D_RWKV), 0.1)
    rwkv_gn_w = 1.0 + nrm(ks[14], (DEPTH, D_RWKV), 0.02)
    rwkv_gn_b = nrm(ks[15], (DEPTH, D_RWKV), 0.02)
    mlstm_conv_w = nrm(ks[16], (DEPTH, CONV_K, 2 * D_MLSTM), CONV_K ** -0.5)
    mlstm_conv_b = nrm(ks[17], (DEPTH, 2 * D_MLSTM), 0.02)
    mlstm_b_i = nrm(ks[18], (DEPTH, MLSTM_HEADS), 0.1)
    mlstm_b_f = jnp.linspace(3.0, 6.0, MLSTM_HEADS, dtype=jnp.float32)[None] + nrm(ks[19], (DEPTH, MLSTM_HEADS), 0.1)
    mlstm_ln_w = 1.0 + nrm(ks[20], (DEPTH, D_MLSTM), 0.02)
    mlstm_skip = 1.0 + nrm(ks[21], (DEPTH, D_MLSTM), 0.02)
    w_out = nrm(ks[22], (DEPTH, D_MIX, D_MODEL), D_MIX ** -0.5)
    final_gain = 1.0 + nrm(ks[23], (D_MODEL,), 0.02)
    return {'x': x, 'c': c, 'w_ada': w_ada, 'b_ada': b_ada, 'norm_gain': norm_gain, 'w_in': w_in,
            'mu_rwkv': mu_rwkv, 'w_decay_up': w_decay_up, 'w_decay0': w_decay0, 'w_icl_up': w_icl_up,
            'a0': a0, 'k_k': k_k, 'k_a': k_a, 'r_k': r_k, 'rwkv_gn_w': rwkv_gn_w, 'rwkv_gn_b': rwkv_gn_b,
            'mlstm_conv_w': mlstm_conv_w, 'mlstm_conv_b': mlstm_conv_b, 'mlstm_b_i': mlstm_b_i,
            'mlstm_b_f': mlstm_b_f, 'mlstm_ln_w': mlstm_ln_w, 'mlstm_skip': mlstm_skip,
            'w_out': w_out, 'final_gain': final_gain}


def reference(x, c, w_ada, b_ada, norm_gain, w_in, mu_rwkv, w_decay_up, w_decay0, w_icl_up,
              a0, k_k, k_a, r_k, rwkv_gn_w, rwkv_gn_b, mlstm_conv_w, mlstm_conv_b, mlstm_b_i,
              mlstm_b_f, mlstm_ln_w, mlstm_skip, w_out, final_gain):
    h_res = x.astype(jnp.float32)
    c_act = jax.nn.silu(c.astype(jnp.float32))
    for l in range(DEPTH):
        ada = c_act @ w_ada[l] + b_ada[l]
        shift, scale, gate = jnp.split(ada, 3, axis=-1)
        hn = _rmsnorm(h_res, norm_gain[l]) * (1.0 + scale[:, None]) + shift[:, None]
        proj = hn @ w_in[l]
        u_rwkv = proj[..., :D_RWKV_SHIFT]
        z_r, q_m, k_m, v_m, o_m, i_m, f_m, z_m = jnp.split(proj[..., D_RWKV_SHIFT:], _offsets(REST_SPLITS), axis=-1)
        y_r = _rwkv7_branch(u_rwkv, z_r, mu_rwkv[l], w_decay_up[l], w_decay0[l], w_icl_up[l], a0[l],
                            k_k[l], k_a[l], r_k[l], rwkv_gn_w[l], rwkv_gn_b[l])
        y_m = _mlstm_branch(q_m, k_m, v_m, o_m, i_m, f_m, z_m, mlstm_conv_w[l], mlstm_conv_b[l],
                            mlstm_b_i[l], mlstm_b_f[l], mlstm_ln_w[l], mlstm_skip[l])
        mix = jnp.concatenate([y_r, y_m], axis=-1) @ w_out[l]
        h_res = h_res + gate[:, None] * mix
    return _rmsnorm(h_res, final_gain).astype(x.dtype)
```

```python
import functools

import jax
import jax.numpy as jnp
from jax import lax
from jax.experimental import pallas as pl
from jax.experimental.pallas import tpu as pltpu

F32 = jnp.float32
BF16 = jnp.bfloat16

RWKV_HEAD = 64
RWKV_HEADS = 8
D_RWKV = RWKV_HEAD * RWKV_HEADS
MLSTM_HEADS = 4
MLSTM_HEAD = 128
D_MLSTM = MLSTM_HEADS * MLSTM_HEAD
LORA = 64
D_SHIFT = 3 * D_RWKV + 2 * LORA
CONV_K = 4
NORM_EPS = 1e-6
RWKV_GN_EPS = 64e-5
MLSTM_LN_EPS = 1e-6
GATE_PAD = 128

RWKV_CHUNK = 64
MLSTM_CHUNK = 128
LANES = 128
VMEM_LIMIT = 56 * 1024 * 1024


def _dot(a, b):
    return jnp.dot(a, b, preferred_element_type=F32)


def _dot_nt(a, b):
    return lax.dot_general(a, b, (((1,), (1,)), ((), ())), preferred_element_type=F32)


def _dot_tn(a, b):
    return lax.dot_general(a, b, (((0,), (0,)), ((), ())), preferred_element_type=F32)


def _split_bf16(x, n):
    parts = []
    rem = x
    for i in range(n):
        p = rem.astype(BF16)
        parts.append(p)
        if i + 1 < n:
            rem = rem - p.astype(F32)
    return parts


def _dot_split_lhs(x, m, n):
    acc = None
    for p in _split_bf16(x, n):
        d = _dot(p, m)
        acc = d if acc is None else acc + d
    return acc


def _dot_split_rhs(m, x, n):
    acc = None
    for p in _split_bf16(x, n):
        d = _dot(m, p)
        acc = d if acc is None else acc + d
    return acc


def _sigmoid(x):
    return 1.0 / (1.0 + jnp.exp(-x))


def _silu(x):
    return x * _sigmoid(x)


def _ada_kernel(c_ref, w_ref, b_ref, o_ref):
    ca = _silu(c_ref[...])
    c_hi, c_lo = _split_bf16(ca, 2)
    w = w_ref[...]
    w_hi, w_lo = _split_bf16(w, 2)
    o_ref[...] = _dot(c_hi, w_hi) + _dot(c_hi, w_lo) + _dot(c_lo, w_hi) + b_ref[...]


def _ada(c, w, b):
    bsz, d = c.shape
    n = w.shape[1]
    bn = d
    return pl.pallas_call(
        _ada_kernel,
        grid=(n // bn,),
        in_specs=[pl.BlockSpec((bsz, d), lambda j: (0, 0)),
                  pl.BlockSpec((d, bn), lambda j: (0, j)),
                  pl.BlockSpec((1, bn), lambda j: (0, j))],
        out_specs=pl.BlockSpec((bsz, bn), lambda j: (0, j)),
        out_shape=jax.ShapeDtypeStruct((bsz, n), F32),
        compiler_params=pltpu.CompilerParams(dimension_semantics=("arbitrary",),
                                             vmem_limit_bytes=VMEM_LIMIT),
        name="ada",
    )(c, w, b.reshape(1, n))


IN_SPLITS = (D_SHIFT, D_RWKV, 2 * D_MLSTM, D_MLSTM, D_MLSTM, D_MLSTM, GATE_PAD)


def _inproj_kernel(x_ref, scale_ref, shift_ref, gain_ref, w_ref, *out_refs):
    x = x_ref[0]
    ms = jnp.mean(x * x, axis=-1, keepdims=True)
    hn = x * lax.rsqrt(ms + NORM_EPS) * gain_ref[...]
    hn = hn * (1.0 + scale_ref[0]) + shift_ref[0]
    hb = hn.astype(BF16)
    off = 0
    for o_ref, n in zip(out_refs, IN_SPLITS):
        o_ref[0] = _dot(hb, w_ref[:, off:off + n]).astype(o_ref.dtype)
        off += n


def _inproj(h_res, scale, shift, gain, w_perm, tm, act_dtype):
    bsz, t, d = h_res.shape
    n_tot = w_perm.shape[1]
    row = lambda b, i: (b, i, 0)
    per_b = lambda b, i: (b, 0, 0)
    const = lambda b, i: (0, 0)
    return pl.pallas_call(
        _inproj_kernel,
        grid=(bsz, t // tm),
        in_specs=[pl.BlockSpec((1, tm, d), row),
                  pl.BlockSpec((1, 1, d), per_b),
                  pl.BlockSpec((1, 1, d), per_b),
                  pl.BlockSpec((1, d), const),
                  pl.BlockSpec((d, n_tot), const)],
        out_specs=[pl.BlockSpec((1, tm, n), row) for n in IN_SPLITS],
        out_shape=[jax.ShapeDtypeStruct((bsz, t, n), act_dtype) for n in IN_SPLITS],
        compiler_params=pltpu.CompilerParams(dimension_semantics=("arbitrary", "arbitrary"),
                                             vmem_limit_bytes=VMEM_LIMIT),
        name="in_proj",
    )(h_res, scale.reshape(bsz, 1, d), shift.reshape(bsz, 1, d), gain.reshape(1, d), w_perm)


def _pair_blockdiag(x):
    lane = lax.broadcasted_iota(jnp.int32, x.shape, 1)
    first = lane < RWKV_HEAD
    zero = jnp.zeros_like(x)
    return jnp.concatenate([jnp.where(first, x, zero), jnp.where(first, zero, x)], axis=0)


def _rwkv_kernel(u_ref, z_ref, mu_ref, lora_ref, w0_ref, a0_ref, kk_ref, ka_ref, rk_ref,
                 gnw_ref, gnb_ref, hsum_ref, tril_ref, triu_ref,
                 y_ref,
                 carry_ref, state_ref, at_ref, rt_ref, bt_ref, kt_ref, bp_ref, kp_ref, vb_ref,
                 gam_ref, yacc_ref, *, tt):
    L = RWKV_CHUNK
    n_pairs = D_RWKV // LANES

    @pl.when(pl.program_id(1) == 0)
    def _():
        carry_ref[...] = jnp.zeros_like(carry_ref)
        state_ref[...] = jnp.zeros_like(state_ref)

    u_raw = u_ref[0].astype(F32)
    row = lax.broadcasted_iota(jnp.int32, u_raw.shape, 0)
    prev = jnp.where(row == 0, carry_ref[...], pltpu.roll(u_raw, 1, 0))
    carry_ref[...] = u_raw[tt - 1:tt, :]
    u = u_raw + mu_ref[...] * (prev - u_raw)

    r = u[:, 0:D_RWKV]
    k = u[:, D_RWKV:2 * D_RWKV]
    v = u[:, 2 * D_RWKV:3 * D_RWKV]
    g = u[:, 3 * D_RWKV:D_SHIFT]
    lw = _dot(jnp.tanh(g).astype(BF16), lora_ref[0])
    la = _dot(g.astype(BF16), lora_ref[1])

    ld = -jnp.exp(F32(-0.5)) * _sigmoid(w0_ref[...] + lw)
    a = _sigmoid(a0_ref[...] + la)
    hsum = hsum_ref[...]
    kk = k * kk_ref[...]
    ss = _dot_split_lhs(kk * kk, hsum, 2)
    kk = kk / jnp.maximum(jnp.sqrt(ss), 1e-12)
    k2 = k * (1.0 + (a - 1.0) * ka_ref[...])
    bonus = _dot_split_lhs(r * k2 * rk_ref[...], hsum, 2) * v
    kka = kk * a

    c = _dot_split_rhs(tril_ref[...], ld, 3)
    rev = _dot_split_rhs(triu_ref[...], ld, 3)
    e_pos = jnp.exp(c)
    e_neg = jnp.exp(-c)
    e_rev = jnp.exp(rev)
    rt_ref[...] = (r * e_pos).astype(BF16)
    kt_ref[...] = (k2 * e_neg).astype(BF16)
    at_ref[...] = (-kk * jnp.exp(c - ld)).astype(BF16)
    bt_ref[...] = (kka * e_neg).astype(BF16)
    kp_ref[...] = (k2 * e_rev).astype(BF16)
    bp_ref[...] = (kka * e_rev).astype(BF16)
    vb_ref[...] = v.astype(BF16)
    gam_ref[...] = jnp.exp(c + rev)

    ti = lax.broadcasted_iota(jnp.int32, (L, LANES), 0)
    si = lax.broadcasted_iota(jnp.int32, (L, LANES), 1) % L
    strict = si < ti
    incl = si <= ti
    eye_pair = (si == ti).astype(F32)
    xr = ti ^ si
    r2 = lax.broadcasted_iota(jnp.int32, (LANES, LANES), 0)
    c2 = lax.broadcasted_iota(jnp.int32, (LANES, LANES), 1)
    same_head = (r2 < RWKV_HEAD) == (c2 < RWKV_HEAD)
    eye_full = r2 == c2

    def chunk_body(ci, carry):
        rows = pl.ds(pl.multiple_of(ci * L, L), L)
        for p in range(n_pairs):
            cols = slice(p * LANES, (p + 1) * LANES)
            at = at_ref[rows, cols]
            rt = rt_ref[rows, cols]
            bt = bt_ref[rows, cols]
            kt = kt_ref[rows, cols]
            bp = bp_ref[rows, cols]
            kp = kp_ref[rows, cols]
            vb = vb_ref[rows, cols]
            aa = _dot_nt(jnp.concatenate([at, rt], axis=0),
                         jnp.concatenate([_pair_blockdiag(bt), _pair_blockdiag(kt)], axis=0))
            zero = jnp.zeros((L, LANES), F32)
            ab = jnp.where(strict, aa[:L, :LANES], zero)
            ak = jnp.where(strict, aa[:L, LANES:], zero)
            rb = jnp.where(incl, aa[L:, :LANES], zero)
            rk = jnp.where(incl, aa[L:, LANES:], zero)
            tinv = eye_pair + jnp.where((xr >> 1) == 0, ab, zero)
            for lvl in range(1, 6):
                a_off = jnp.where((xr >> lvl) == 1, ab, zero).astype(BF16)
                yy = _dot(a_off, _pair_blockdiag(tinv.astype(BF16)))
                tinv = tinv + _dot(tinv.astype(BF16), _pair_blockdiag(yy.astype(BF16)))
            vbd = _pair_blockdiag(vb)
            w = _dot(ak.astype(BF16), vbd)
            ap = _dot(tinv.astype(BF16),
                      jnp.concatenate([_pair_blockdiag(at), _pair_blockdiag(w.astype(BF16))], axis=1))
            ahat = ap[:, :LANES].astype(BF16)
            p1 = ap[:, LANES:].astype(BF16)
            q = _dot(rb.astype(BF16),
                     jnp.concatenate([_pair_blockdiag(ahat), _pair_blockdiag(p1)], axis=1))
            rhat = rt.astype(F32) + q[:, :LANES]
            y_in = q[:, LANES:] + _dot(rk.astype(BF16), vbd)
            mg = _dot_tn(jnp.concatenate([bp, kp], axis=0),
                         jnp.concatenate([jnp.concatenate([ahat, p1], axis=1),
                                          jnp.concatenate([jnp.zeros_like(vb), vb], axis=1)], axis=0))
            zero2 = jnp.zeros((LANES, LANES), F32)
            mx = jnp.where(same_head, mg[:, :LANES], zero2)
            gg = jnp.where(same_head, mg[:, LANES:], zero2)
            h = state_ref[p]
            yh = _dot(jnp.concatenate([rhat, mx], axis=0).astype(BF16), h.astype(BF16))
            yacc_ref[rows, cols] = yh[:L] + y_in
            gam_row = gam_ref[pl.ds(pl.multiple_of(ci * L, L), 1), cols]
            gam_col = jnp.sum(jnp.where(eye_full, jnp.broadcast_to(gam_row, (LANES, LANES)), zero2),
                              axis=1, keepdims=True)
            state_ref[p] = gam_col * h + yh[L:] + gg
        return carry

    lax.fori_loop(0, tt // L, chunk_body, 0)

    y = yacc_ref[...]
    inv_n = F32(1.0 / RWKV_HEAD)
    mean = _dot_split_lhs(y, hsum, 2) * inv_n
    d = y - mean
    var = _dot_split_lhs(d * d, hsum, 2) * inv_n
    yn = d * lax.rsqrt(var + RWKV_GN_EPS) * gnw_ref[...] + gnb_ref[...]
    y_ref[0] = ((yn + bonus) * _silu(z_ref[0].astype(F32))).astype(y_ref.dtype)


def _chunk_tri(n, chunk, kind):
    i = jnp.arange(n)[:, None]
    j = jnp.arange(n)[None, :]
    same = (i // chunk) == (j // chunk)
    m = same & ((j <= i) if kind == "lower_incl" else (j > i))
    return m.astype(BF16)


def _rwkv(u, z, p, tt, out_dtype):
    bsz, t, _ = u.shape
    row = lambda b, i: (b, i, 0)
    c2 = lambda b, i: (0, 0)
    c3 = lambda b, i: (0, 0, 0)
    vec = lambda n: pl.BlockSpec((1, n), c2)
    hsum = ((jnp.arange(D_RWKV)[:, None] // RWKV_HEAD) ==
            (jnp.arange(D_RWKV)[None, :] // RWKV_HEAD)).astype(BF16)
    lora = jnp.stack([
        jnp.concatenate([p["w_decay_up"], jnp.zeros((LORA, D_RWKV), F32)], axis=0),
        jnp.concatenate([jnp.zeros((LORA, D_RWKV), F32), p["w_icl_up"]], axis=0)]).astype(BF16)
    v1 = lambda a: a.reshape(1, -1).astype(F32)
    act = lambda: pltpu.VMEM((tt, D_RWKV), BF16)
    return pl.pallas_call(
        functools.partial(_rwkv_kernel, tt=tt),
        grid=(bsz, t // tt),
        in_specs=[pl.BlockSpec((1, tt, D_SHIFT), row),
                  pl.BlockSpec((1, tt, D_RWKV), row),
                  vec(D_SHIFT),
                  pl.BlockSpec((2, 2 * LORA, D_RWKV), c3),
                  vec(D_RWKV), vec(D_RWKV), vec(D_RWKV), vec(D_RWKV), vec(D_RWKV),
                  vec(D_RWKV), vec(D_RWKV),
                  pl.BlockSpec((D_RWKV, D_RWKV), c2),
                  pl.BlockSpec((tt, tt), c2),
                  pl.BlockSpec((tt, tt), c2)],
        out_specs=pl.BlockSpec((1, tt, D_RWKV), row),
        out_shape=jax.ShapeDtypeStruct((bsz, t, D_RWKV), out_dtype),
        scratch_shapes=[pltpu.VMEM((1, D_SHIFT), F32),
                        pltpu.VMEM((D_RWKV // LANES, LANES, LANES), F32),
                        act(), act(), act(), act(), act(), act(), act(),
                        pltpu.VMEM((tt, D_RWKV), F32),
                        pltpu.VMEM((tt, D_RWKV), F32)],
        compiler_params=pltpu.CompilerParams(dimension_semantics=("arbitrary", "arbitrary"),
                                             vmem_limit_bytes=VMEM_LIMIT),
        name="rwkv7",
    )(u, z, v1(p["mu_rwkv"]), lora, v1(p["w_decay0"]), v1(p["a0"]), v1(p["k_k"]), v1(p["k_a"]),
      v1(p["r_k"]), v1(p["rwkv_gn_w"]), v1(p["rwkv_gn_b"]), hsum,
      _chunk_tri(tt, RWKV_CHUNK, "lower_incl"), _chunk_tri(tt, RWKV_CHUNK, "upper_strict"))


def _mlstm_kernel(qk_ref, v_ref, o_ref, z_ref, g_ref, convw_ref, convb_ref, gbias_ref, lnw_ref,
                  skip_ref, tril_ref,
                  y_ref,
                  xbuf_ref, state_ref, m_ref, *, tt):
    L = MLSTM_CHUNK
    H = MLSTM_HEADS
    dh = MLSTM_HEAD

    @pl.when(pl.program_id(1) == 0)
    def _():
        xbuf_ref[0:8, :] = jnp.zeros((8, 2 * D_MLSTM), F32)
        state_ref[...] = jnp.zeros_like(state_ref)
        m_ref[...] = jnp.zeros_like(m_ref)

    xbuf_ref[8:8 + tt, :] = qk_ref[0].astype(F32)
    xb = xbuf_ref[...]
    acc = convb_ref[...] + convw_ref[CONV_K - 1:CONV_K, :] * xb[8:8 + tt]
    for j in range(CONV_K - 1):
        sh = CONV_K - 1 - j
        acc = acc + convw_ref[j:j + 1, :] * pltpu.roll(xb, sh, 0)[8:8 + tt]
    xbuf_ref[0:8, :] = xb[tt:tt + 8]
    qk = _silu(acc)
    q_c = qk[:, :D_MLSTM]
    k_c = qk[:, D_MLSTM:] * F32(dh ** -0.5)

    gt = g_ref[0].astype(F32) + gbias_ref[...]
    lane = lax.broadcasted_iota(jnp.int32, gt.shape, 1)
    is_f = (lane >= H) & (lane < 2 * H)
    log_f = jnp.minimum(gt, 0.0) - jnp.log(1.0 + jnp.exp(-jnp.abs(gt)))
    gates = jnp.where(is_f, log_f, gt)
    bcum = _dot_split_rhs(tril_ref[...], gates, 3)

    ti = lax.broadcasted_iota(jnp.int32, (L, L), 0)
    si = lax.broadcasted_iota(jnp.int32, (L, L), 1)
    causal = si <= ti
    ones_l = jnp.ones((L, dh), BF16)

    vv = v_ref[0]
    oo = o_ref[0]
    zz = z_ref[0]
    for ci in range(tt // L):
        rows = slice(ci * L, (ci + 1) * L)
        g_rows = jnp.transpose(gates[rows])
        b_rows = jnp.transpose(bcum[rows])
        for h in range(H):
            cols = slice(h * dh, (h + 1) * dh)
            q = q_c[rows, cols]
            kh = k_c[rows, cols]
            v_aug = jnp.concatenate([vv[rows, cols].astype(BF16), ones_l], axis=1)
            b_col = bcum[rows, H + h:H + h + 1]
            i_col = gates[rows, h:h + 1]
            b_row = b_rows[H + h:H + h + 1, :]
            i_row = g_rows[h:h + 1, :]
            m_prev = m_ref[h:h + 1, 0:1]
            d_log = jnp.where(causal, b_col - b_row + i_row, -jnp.inf)
            inter_log = b_col + m_prev
            m_t = jnp.maximum(inter_log, jnp.max(d_log, axis=-1, keepdims=True))
            d_w = jnp.exp(d_log - m_t)
            scores = (_dot_nt(q.astype(BF16), kh.astype(BF16)) * d_w).astype(BF16)
            inter = jnp.exp(inter_log - m_t)
            st = state_ref[h]
            nd = _dot(scores, v_aug) + inter * _dot(q.astype(BF16), st.astype(BF16))
            num = nd[:, :dh]
            den = nd[:, dh:]
            hh = num / jnp.maximum(jnp.abs(den), jnp.exp(-m_t))
            b_last = b_col[L - 1:L, :]
            m_new = m_t[L - 1:L, :]
            w_s = jnp.exp(b_last - b_col + i_col - m_new)
            cs = jnp.exp(b_last + m_prev - m_new)
            state_ref[h] = cs * st + _dot_tn((kh * w_s).astype(BF16), v_aug)
            m_ref[h:h + 1, :] = jnp.broadcast_to(m_new, (1, LANES))
            mean = jnp.mean(hh, axis=-1, keepdims=True)
            dv = hh - mean
            var = jnp.mean(dv * dv, axis=-1, keepdims=True)
            hn = dv * lax.rsqrt(var + MLSTM_LN_EPS) * lnw_ref[:, cols]
            out = _sigmoid(oo[rows, cols].astype(F32)) * hn + skip_ref[:, cols] * q
            y_ref[0, rows, cols] = (out * _silu(zz[rows, cols].astype(F32))).astype(y_ref.dtype)


def _mlstm(qk, v, o, z, g, p, tt, out_dtype):
    bsz, t, _ = qk.shape
    row = lambda b, i: (b, i, 0)
    c2 = lambda b, i: (0, 0)
    vec = lambda n: pl.BlockSpec((1, n), c2)
    v1 = lambda a: a.reshape(1, -1).astype(F32)
    gbias = jnp.concatenate([p["mlstm_b_i"], p["mlstm_b_f"],
                             jnp.zeros((GATE_PAD - 2 * MLSTM_HEADS,), F32)]).reshape(1, GATE_PAD)
    return pl.pallas_call(
        functools.partial(_mlstm_kernel, tt=tt),
        grid=(bsz, t // tt),
        in_specs=[pl.BlockSpec((1, tt, 2 * D_MLSTM), row),
                  pl.BlockSpec((1, tt, D_MLSTM), row),
                  pl.BlockSpec((1, tt, D_MLSTM), row),
                  pl.BlockSpec((1, tt, D_MLSTM), row),
                  pl.BlockSpec((1, tt, GATE_PAD), row),
                  pl.BlockSpec((CONV_K, 2 * D_MLSTM), c2),
                  vec(2 * D_MLSTM), vec(GATE_PAD), vec(D_MLSTM), vec(D_MLSTM),
                  pl.BlockSpec((tt, tt), c2)],
        out_specs=pl.BlockSpec((1, tt, D_MLSTM), row),
        out_shape=jax.ShapeDtypeStruct((bsz, t, D_MLSTM), out_dtype),
        scratch_shapes=[pltpu.VMEM((tt + 8, 2 * D_MLSTM), F32),
                        pltpu.VMEM((MLSTM_HEADS, MLSTM_HEAD, 2 * MLSTM_HEAD), F32),
                        pltpu.VMEM((8, LANES), F32)],
        compiler_params=pltpu.CompilerParams(dimension_semantics=("arbitrary", "arbitrary"),
                                             vmem_limit_bytes=VMEM_LIMIT),
        name="mlstm",
    )(qk, v, o, z, g, p["mlstm_conv_w"].astype(F32), v1(p["mlstm_conv_b"]), gbias,
      v1(p["mlstm_ln_w"]), v1(p["mlstm_skip"]), _chunk_tri(tt, MLSTM_CHUNK, "lower_incl"))


def _outproj_kernel(yr_ref, ym_ref, x_ref, gate_ref, w_ref, fg_ref, o_ref, *, final):
    mix = _dot(yr_ref[0], w_ref[0:D_RWKV, :]) + _dot(ym_ref[0], w_ref[D_RWKV:, :])
    h = x_ref[0] + gate_ref[0] * mix
    if final:
        ms = jnp.mean(h * h, axis=-1, keepdims=True)
        h = h * lax.rsqrt(ms + NORM_EPS) * fg_ref[...]
    o_ref[0] = h.astype(o_ref.dtype)


def _outproj(y_r, y_m, h_res, gate, w_out, final_gain, tm, final, out_dtype):
    bsz, t, d = h_res.shape
    row = lambda b, i: (b, i, 0)
    per_b = lambda b, i: (b, 0, 0)
    const = lambda b, i: (0, 0)
    return pl.pallas_call(
        functools.partial(_outproj_kernel, final=final),
        grid=(bsz, t // tm),
        in_specs=[pl.BlockSpec((1, tm, D_RWKV), row),
                  pl.BlockSpec((1, tm, D_MLSTM), row),
                  pl.BlockSpec((1, tm, d), row),
                  pl.BlockSpec((1, 1, d), per_b),
                  pl.BlockSpec((D_RWKV + D_MLSTM, d), const),
                  pl.BlockSpec((1, d), const)],
        out_specs=pl.BlockSpec((1, tm, d), row),
        out_shape=jax.ShapeDtypeStruct((bsz, t, d), out_dtype),
        compiler_params=pltpu.CompilerParams(dimension_semantics=("arbitrary", "arbitrary"),
                                             vmem_limit_bytes=VMEM_LIMIT),
        name="out_proj",
    )(y_r, y_m, h_res, gate.reshape(bsz, 1, d), w_out, final_gain.reshape(1, d))


def _permute_w_in(w):
    o = D_SHIFT
    z_r = w[:, o:o + D_RWKV]; o += D_RWKV
    q_m = w[:, o:o + D_MLSTM]; o += D_MLSTM
    k_m = w[:, o:o + D_MLSTM]; o += D_MLSTM
    v_m = w[:, o:o + D_MLSTM]; o += D_MLSTM
    o_m = w[:, o:o + D_MLSTM]; o += D_MLSTM
    i_m = w[:, o:o + MLSTM_HEADS]; o += MLSTM_HEADS
    f_m = w[:, o:o + MLSTM_HEADS]; o += MLSTM_HEADS
    z_m = w[:, o:o + D_MLSTM]
    pad = jnp.zeros((w.shape[0], GATE_PAD - 2 * MLSTM_HEADS), w.dtype)
    return jnp.concatenate([w[:, :D_SHIFT], z_r, q_m, k_m, v_m, o_m, z_m, i_m, f_m, pad],
                           axis=1).astype(BF16)


def _tile(t, want):
    tile = min(t, want)
    assert t % tile == 0
    return tile


def kernel(x, c, w_ada, b_ada, norm_gain, w_in, mu_rwkv, w_decay_up, w_decay0, w_icl_up, a0, k_k, k_a, r_k, rwkv_gn_w, rwkv_gn_b, mlstm_conv_w, mlstm_conv_b, mlstm_b_i, mlstm_b_f, mlstm_ln_w, mlstm_skip, w_out, final_gain):
    bsz, t, d = x.shape
    depth = w_ada.shape[0]
    act_dtype = F32
    tm = _tile(t, 256)
    tt_r = _tile(t, 256)
    tt_m = _tile(t, 256)
    assert t % RWKV_CHUNK == 0 and t % MLSTM_CHUNK == 0
    h_res = x.astype(F32)
    c32 = c.astype(F32)
    for l in range(depth):
        ada = _ada(c32, w_ada[l], b_ada[l])
        shift, scale, gate = ada[:, :d], ada[:, d:2 * d], ada[:, 2 * d:]
        u, z_r, qk_m, v_m, o_m, z_m, g_m = _inproj(h_res, scale, shift, norm_gain[l],
                                                   _permute_w_in(w_in[l]), tm, act_dtype)
        rp = dict(mu_rwkv=mu_rwkv[l], w_decay_up=w_decay_up[l], w_decay0=w_decay0[l],
                  w_icl_up=w_icl_up[l], a0=a0[l], k_k=k_k[l], k_a=k_a[l], r_k=r_k[l],
                  rwkv_gn_w=rwkv_gn_w[l], rwkv_gn_b=rwkv_gn_b[l])
        y_r = _rwkv(u, z_r, rp, tt_r, BF16)
        mp = dict(mlstm_conv_w=mlstm_conv_w[l], mlstm_conv_b=mlstm_conv_b[l], mlstm_b_i=mlstm_b_i[l],
                  mlstm_b_f=mlstm_b_f[l], mlstm_ln_w=mlstm_ln_w[l], mlstm_skip=mlstm_skip[l])
        y_m = _mlstm(qk_m, v_m, o_m, z_m, g_m, mp, tt_m, BF16)
        final = l == depth - 1
        h_res = _outproj(y_r, y_m, h_res, gate, w_out[l].astype(BF16), final_gain, tm, final,
                         x.dtype if final else F32)
    return h_res
```

```python
import functools

import jax
import jax.numpy as jnp
from jax import lax
from jax.experimental import pallas as pl
from jax.experimental.pallas import tpu as pltpu

F32 = jnp.float32
BF16 = jnp.bfloat16

RWKV_HEAD = 64
RWKV_HEADS = 8
D_RWKV = RWKV_HEAD * RWKV_HEADS
MLSTM_HEADS = 4
MLSTM_HEAD = 128
D_MLSTM = MLSTM_HEADS * MLSTM_HEAD
LORA = 64
D_SHIFT = 3 * D_RWKV + 2 * LORA
CONV_K = 4
NORM_EPS = 1e-6
RWKV_GN_EPS = 64e-5
MLSTM_LN_EPS = 1e-6
GATE_PAD = 128

RWKV_CHUNK = 64
MLSTM_CHUNK = 128
LANES = 128
VMEM_LIMIT = 56 * 1024 * 1024


def _dot(a, b):
    return jnp.dot(a, b, preferred_element_type=F32)


def _dot_nt(a, b):
    return lax.dot_general(a, b, (((1,), (1,)), ((), ())), preferred_element_type=F32)


def _dot_tn(a, b):
    return lax.dot_general(a, b, (((0,), (0,)), ((), ())), preferred_element_type=F32)


def _split_bf16(x, n):
    parts = []
    rem = x
    for i in range(n):
        p = rem.astype(BF16)
        parts.append(p)
        if i + 1 < n:
            rem = rem - p.astype(F32)
    return parts


def _dot_split_lhs(x, m, n):
    acc = None
    for p in _split_bf16(x, n):
        d = _dot(p, m)
        acc = d if acc is None else acc + d
    return acc


def _dot_split_rhs(m, x, n):
    acc = None
    for p in _split_bf16(x, n):
        d = _dot(m, p)
        acc = d if acc is None else acc + d
    return acc


def _sigmoid(x):
    return 1.0 / (1.0 + jnp.exp(-x))


def _silu(x):
    return x * _sigmoid(x)


def _ada_kernel(c_ref, w_ref, b_ref, o_ref):
    ca = _silu(c_ref[...])
    c_hi, c_lo = _split_bf16(ca, 2)
    w = w_ref[...]
    w_hi, w_lo = _split_bf16(w, 2)
    o_ref[...] = _dot(c_hi, w_hi) + _dot(c_hi, w_lo) + _dot(c_lo, w_hi) + b_ref[...]


def _ada(c, w, b):
    bsz, d = c.shape
    n = w.shape[1]
    bn = d
    return pl.pallas_call(
        _ada_kernel,
        grid=(n // bn,),
        in_specs=[pl.BlockSpec((bsz, d), lambda j: (0, 0)),
                  pl.BlockSpec((d, bn), lambda j: (0, j)),
                  pl.BlockSpec((1, bn), lambda j: (0, j))],
        out_specs=pl.BlockSpec((bsz, bn), lambda j: (0, j)),
        out_shape=jax.ShapeDtypeStruct((bsz, n), F32),
        compiler_params=pltpu.CompilerParams(dimension_semantics=("arbitrary",),
                                             vmem_limit_bytes=VMEM_LIMIT),
        name="ada",
    )(c, w, b.reshape(1, n))


IN_SPLITS = (D_SHIFT, D_RWKV, 2 * D_MLSTM, D_MLSTM, D_MLSTM, D_MLSTM, GATE_PAD)


def _inproj_kernel(x_ref, scale_ref, shift_ref, gain_ref, w_ref, *out_refs):
    x = x_ref[0]
    ms = jnp.mean(x * x, axis=-1, keepdims=True)
    hn = x * lax.rsqrt(ms + NORM_EPS) * gain_ref[...]
    hn = hn * (1.0 + scale_ref[0]) + shift_ref[0]
    hb = hn.astype(BF16)
    off = 0
    for o_ref, n in zip(out_refs, IN_SPLITS):
        o_ref[0] = _dot(hb, w_ref[:, off:off + n]).astype(o_ref.dtype)
        off += n


def _inproj(h_res, scale, shift, gain, w_perm, tm, act_dtype):
    bsz, t, d = h_res.shape
    n_tot = w_perm.shape[1]
    row = lambda b, i: (b, i, 0)
    per_b = lambda b, i: (b, 0, 0)
    const = lambda b, i: (0, 0)
    return pl.pallas_call(
        _inproj_kernel,
        grid=(bsz, t // tm),
        in_specs=[pl.BlockSpec((1, tm, d), row),
                  pl.BlockSpec((1, 1, d), per_b),
                  pl.BlockSpec((1, 1, d), per_b),
                  pl.BlockSpec((1, d), const),
                  pl.BlockSpec((d, n_tot), const)],
        out_specs=[pl.BlockSpec((1, tm, n), row) for n in IN_SPLITS],
        out_shape=[jax.ShapeDtypeStruct((bsz, t, n), act_dtype) for n in IN_SPLITS],
        compiler_params=pltpu.CompilerParams(dimension_semantics=("arbitrary", "arbitrary"),
                                             vmem_limit_bytes=VMEM_LIMIT),
        name="in_proj",
    )(h_res, scale.reshape(bsz, 1, d), shift.reshape(bsz, 1, d), gain.reshape(1, d), w_perm)


def _pair_blockdiag(x):
    lane = lax.broadcasted_iota(jnp.int32, x.shape, 1)
    first = lane < RWKV_HEAD
    zero = jnp.zeros_like(x)
    return jnp.concatenate([jnp.where(first, x, zero), jnp.where(first, zero, x)], axis=0)


def _rwkv_kernel(u_ref, z_ref, mu_ref, lora_ref, w0_ref, a0_ref, kk_ref, ka_ref, rk_ref,
                 gnw_ref, gnb_ref, hsum_ref, tril_ref, triu_ref,
                 y_ref,
                 carry_ref, state_ref, at_ref, rt_ref, bt_ref, kt_ref, bp_ref, kp_ref, vb_ref,
                 gam_ref, yacc_ref, *, tt):
    L = RWKV_CHUNK
    n_pairs = D_RWKV // LANES

    @pl.when(pl.program_id(1) == 0)
    def _():
        carry_ref[...] = jnp.zeros_like(carry_ref)
        state_ref[...] = jnp.zeros_like(state_ref)

    u_raw = u_ref[0].astype(F32)
    row = lax.broadcasted_iota(jnp.int32, u_raw.shape, 0)
    prev = jnp.where(row == 0, carry_ref[...], pltpu.roll(u_raw, 1, 0))
    carry_ref[...] = u_raw[tt - 1:tt, :]
    u = u_raw + mu_ref[...] * (prev - u_raw)

    r = u[:, 0:D_RWKV]
    k = u[:, D_RWKV:2 * D_RWKV]
    v = u[:, 2 * D_RWKV:3 * D_RWKV]
    g = u[:, 3 * D_RWKV:D_SHIFT]
    lw = _dot(jnp.tanh(g).astype(BF16), lora_ref[0])
    la = _dot(g.astype(BF16), lora_ref[1])

    ld = -jnp.exp(F32(-0.5)) * _sigmoid(w0_ref[...] + lw)
    a = _sigmoid(a0_ref[...] + la)
    hsum = hsum_ref[...]
    kk = k * kk_ref[...]
    ss = _dot_split_lhs(kk * kk, hsum, 2)
    kk = kk / jnp.maximum(jnp.sqrt(ss), 1e-12)
    k2 = k * (1.0 + (a - 1.0) * ka_ref[...])
    bonus = _dot_split_lhs(r * k2 * rk_ref[...], hsum, 2) * v
    kka = kk * a

    c = _dot_split_rhs(tril_ref[...], ld, 3)
    rev = _dot_split_rhs(triu_ref[...], ld, 3)
    e_pos = jnp.exp(c)
    e_neg = jnp.exp(-c)
    e_rev = jnp.exp(rev)
    rt_ref[...] = (r * e_pos).astype(BF16)
    kt_ref[...] = (k2 * e_neg).astype(BF16)
    at_ref[...] = (-kk * jnp.exp(c - ld)).astype(BF16)
    bt_ref[...] = (kka * e_neg).astype(BF16)
    kp_ref[...] = (k2 * e_rev).astype(BF16)
    bp_ref[...] = (kka * e_rev).astype(BF16)
    vb_ref[...] = v.astype(BF16)
    gam_ref[...] = jnp.exp(c + rev)

    ti = lax.broadcasted_iota(jnp.int32, (L, LANES), 0)
    si = lax.broadcasted_iota(jnp.int32, (L, LANES), 1) % L
    strict = si < ti
    incl = si <= ti
    eye_pair = (si == ti).astype(F32)
    xr = ti ^ si
    r2 = lax.broadcasted_iota(jnp.int32, (LANES, LANES), 0)
    c2 = lax.broadcasted_iota(jnp.int32, (LANES, LANES), 1)
    same_head = (r2 < RWKV_HEAD) == (c2 < RWKV_HEAD)
    eye_full = r2 == c2

    n_chunks = tt // L
    chains = [(ci, p) for p in range(n_pairs) for ci in range(n_chunks)]

    def load(ref):
        return [ref[ci * L:(ci + 1) * L, p * LANES:(p + 1) * LANES] for ci, p in chains]

    at, rt, bt, kt, bp, kp, vb = (load(ref) for ref in
                                  (at_ref, rt_ref, bt_ref, kt_ref, bp_ref, kp_ref, vb_ref))
    zero = jnp.zeros((L, LANES), F32)
    zero2 = jnp.zeros((LANES, LANES), F32)
    aa = [_dot_nt(jnp.concatenate([a_, r_], axis=0),
                  jnp.concatenate([_pair_blockdiag(b_), _pair_blockdiag(k_)], axis=0))
          for a_, r_, b_, k_ in zip(at, rt, bt, kt)]
    ab = [jnp.where(strict, x[:L, :LANES], zero) for x in aa]
    ak = [jnp.where(strict, x[:L, LANES:], zero).astype(BF16) for x in aa]
    rb = [jnp.where(incl, x[L:, :LANES], zero).astype(BF16) for x in aa]
    rk = [jnp.where(incl, x[L:, LANES:], zero).astype(BF16) for x in aa]
    tinv = [eye_pair + jnp.where((xr >> 1) == 0, x, zero) for x in ab]
    for lvl in range(1, 6):
        yy = [_dot(jnp.where((xr >> lvl) == 1, x, zero).astype(BF16), _pair_blockdiag(t.astype(BF16)))
              for x, t in zip(ab, tinv)]
        tinv = [t + _dot(t.astype(BF16), _pair_blockdiag(y.astype(BF16))) for t, y in zip(tinv, yy)]
    vbd = [_pair_blockdiag(x) for x in vb]
    w = [_dot(x, y).astype(BF16) for x, y in zip(ak, vbd)]
    ap = [_dot(t.astype(BF16), jnp.concatenate([_pair_blockdiag(a_), _pair_blockdiag(w_)], axis=1))
          for t, a_, w_ in zip(tinv, at, w)]
    ahat = [x[:, :LANES].astype(BF16) for x in ap]
    p1 = [x[:, LANES:].astype(BF16) for x in ap]
    q = [_dot(x, jnp.concatenate([_pair_blockdiag(a_), _pair_blockdiag(p_)], axis=1))
         for x, a_, p_ in zip(rb, ahat, p1)]
    y_in = [q_[:, LANES:] + _dot(x, y) for q_, x, y in zip(q, rk, vbd)]
    mg = [_dot_tn(jnp.concatenate([b_, k_], axis=0),
                  jnp.concatenate([jnp.concatenate([a_, p_], axis=1),
                                   jnp.concatenate([jnp.zeros_like(v_), v_], axis=1)], axis=0))
          for b_, k_, a_, p_, v_ in zip(bp, kp, ahat, p1, vb)]
    lhs = [jnp.concatenate([r_.astype(F32) + q_[:, :LANES], jnp.where(same_head, m_[:, :LANES], zero2)],
                           axis=0).astype(BF16) for r_, q_, m_ in zip(rt, q, mg)]
    gg = [jnp.where(same_head, m_[:, LANES:], zero2) for m_ in mg]
    gam_col = [jnp.sum(jnp.where(eye_full,
                                 jnp.broadcast_to(gam_ref[ci * L:ci * L + 1, p * LANES:(p + 1) * LANES],
                                                  (LANES, LANES)), zero2), axis=1, keepdims=True)
               for ci, p in chains]

    hs = [state_ref[p] for p in range(n_pairs)]
    for ci in range(n_chunks):
        idx = [p * n_chunks + ci for p in range(n_pairs)]
        yh = [_dot(lhs[i], h.astype(BF16)) for i, h in zip(idx, hs)]
        for p, i in enumerate(idx):
            yacc_ref[ci * L:(ci + 1) * L, p * LANES:(p + 1) * LANES] = yh[p][:L] + y_in[i]
        hs = [gam_col[i] * h + y_[L:] + gg[i] for i, h, y_ in zip(idx, hs, yh)]
    for p in range(n_pairs):
        state_ref[p] = hs[p]

    y = yacc_ref[...]
    inv_n = F32(1.0 / RWKV_HEAD)
    mean = _dot_split_lhs(y, hsum, 2) * inv_n
    d = y - mean
    var = _dot_split_lhs(d * d, hsum, 2) * inv_n
    yn = d * lax.rsqrt(var + RWKV_GN_EPS) * gnw_ref[...] + gnb_ref[...]
    y_ref[0] = ((yn + bonus) * _silu(z_ref[0].astype(F32))).astype(y_ref.dtype)


def _chunk_tri(n, chunk, kind):
    i = jnp.arange(n)[:, None]
    j = jnp.arange(n)[None, :]
    same = (i // chunk) == (j // chunk)
    m = same & ((j <= i) if kind == "lower_incl" else (j > i))
    return m.astype(BF16)


def _rwkv(u, z, p, tt, out_dtype):
    bsz, t, _ = u.shape
    row = lambda b, i: (b, i, 0)
    c2 = lambda b, i: (0, 0)
    c3 = lambda b, i: (0, 0, 0)
    vec = lambda n: pl.BlockSpec((1, n), c2)
    hsum = ((jnp.arange(D_RWKV)[:, None] // RWKV_HEAD) ==
            (jnp.arange(D_RWKV)[None, :] // RWKV_HEAD)).astype(BF16)
    lora = jnp.stack([
        jnp.concatenate([p["w_decay_up"], jnp.zeros((LORA, D_RWKV), F32)], axis=0),
        jnp.concatenate([jnp.zeros((LORA, D_RWKV), F32), p["w_icl_up"]], axis=0)]).astype(BF16)
    v1 = lambda a: a.reshape(1, -1).astype(F32)
    act = lambda: pltpu.VMEM((tt, D_RWKV), BF16)
    return pl.pallas_call(
        functools.partial(_rwkv_kernel, tt=tt),
        grid=(bsz, t // tt),
        in_specs=[pl.BlockSpec((1, tt, D_SHIFT), row),
                  pl.BlockSpec((1, tt, D_RWKV), row),
                  vec(D_SHIFT),
                  pl.BlockSpec((2, 2 * LORA, D_RWKV), c3),
                  vec(D_RWKV), vec(D_RWKV), vec(D_RWKV), vec(D_RWKV), vec(D_RWKV),
                  vec(D_RWKV), vec(D_RWKV),
                  pl.BlockSpec((D_RWKV, D_RWKV), c2),
                  pl.BlockSpec((tt, tt), c2),
                  pl.BlockSpec((tt, tt), c2)],
        out_specs=pl.BlockSpec((1, tt, D_RWKV), row),
        out_shape=jax.ShapeDtypeStruct((bsz, t, D_RWKV), out_dtype),
        scratch_shapes=[pltpu.VMEM((1, D_SHIFT), F32),
                        pltpu.VMEM((D_RWKV // LANES, LANES, LANES), F32),
                        act(), act(), act(), act(), act(), act(), act(),
                        pltpu.VMEM((tt, D_RWKV), F32),
                        pltpu.VMEM((tt, D_RWKV), F32)],
        compiler_params=pltpu.CompilerParams(dimension_semantics=("arbitrary", "arbitrary"),
                                             vmem_limit_bytes=VMEM_LIMIT),
        name="rwkv7",
    )(u, z, v1(p["mu_rwkv"]), lora, v1(p["w_decay0"]), v1(p["a0"]), v1(p["k_k"]), v1(p["k_a"]),
      v1(p["r_k"]), v1(p["rwkv_gn_w"]), v1(p["rwkv_gn_b"]), hsum,
      _chunk_tri(tt, RWKV_CHUNK, "lower_incl"), _chunk_tri(tt, RWKV_CHUNK, "upper_strict"))


def _mlstm_kernel(qk_ref, v_ref, o_ref, z_ref, g_ref, convw_ref, convb_ref, gbias_ref, lnw_ref,
                  skip_ref, tril_ref,
                  y_ref,
                  xbuf_ref, state_ref, m_ref, *, tt):
    L = MLSTM_CHUNK
    H = MLSTM_HEADS
    dh = MLSTM_HEAD

    @pl.when(pl.program_id(1) == 0)
    def _():
        xbuf_ref[0:8, :] = jnp.zeros((8, 2 * D_MLSTM), F32)
        state_ref[...] = jnp.zeros_like(state_ref)
        m_ref[...] = jnp.zeros_like(m_ref)

    xbuf_ref[8:8 + tt, :] = qk_ref[0].astype(F32)
    xb = xbuf_ref[...]
    acc = convb_ref[...] + convw_ref[CONV_K - 1:CONV_K, :] * xb[8:8 + tt]
    for j in range(CONV_K - 1):
        sh = CONV_K - 1 - j
        acc = acc + convw_ref[j:j + 1, :] * pltpu.roll(xb, sh, 0)[8:8 + tt]
    xbuf_ref[0:8, :] = xb[tt:tt + 8]
    qk = _silu(acc)
    q_c = qk[:, :D_MLSTM]
    k_c = qk[:, D_MLSTM:] * F32(dh ** -0.5)

    gt = g_ref[0].astype(F32) + gbias_ref[...]
    lane = lax.broadcasted_iota(jnp.int32, gt.shape, 1)
    is_f = (lane >= H) & (lane < 2 * H)
    log_f = jnp.minimum(gt, 0.0) - jnp.log(1.0 + jnp.exp(-jnp.abs(gt)))
    gates = jnp.where(is_f, log_f, gt)
    bcum = _dot_split_rhs(tril_ref[...], gates, 3)

    ti = lax.broadcasted_iota(jnp.int32, (L, L), 0)
    si = lax.broadcasted_iota(jnp.int32, (L, L), 1)
    causal = si <= ti
    ones_l = jnp.ones((L, dh), BF16)

    vv = v_ref[0]
    oo = o_ref[0]
    zz = z_ref[0]
    for ci in range(tt // L):
        rows = slice(ci * L, (ci + 1) * L)
        g_rows = jnp.transpose(gates[rows])
        b_rows = jnp.transpose(bcum[rows])
        for h in range(H):
            cols = slice(h * dh, (h + 1) * dh)
            q = q_c[rows, cols]
            kh = k_c[rows, cols]
            v_aug = jnp.concatenate([vv[rows, cols].astype(BF16), ones_l], axis=1)
            b_col = bcum[rows, H + h:H + h + 1]
            i_col = gates[rows, h:h + 1]
            b_row = b_rows[H + h:H + h + 1, :]
            i_row = g_rows[h:h + 1, :]
            m_prev = m_ref[h:h + 1, 0:1]
            d_log = jnp.where(causal, b_col - b_row + i_row, -jnp.inf)
            inter_log = b_col + m_prev
            m_t = jnp.maximum(inter_log, jnp.max(d_log, axis=-1, keepdims=True))
            d_w = jnp.exp(d_log - m_t)
            scores = (_dot_nt(q.astype(BF16), kh.astype(BF16)) * d_w).astype(BF16)
            inter = jnp.exp(inter_log - m_t)
            st = state_ref[h]
            nd = _dot(scores, v_aug) + inter * _dot(q.astype(BF16), st.astype(BF16))
            num = nd[:, :dh]
            den = nd[:, dh:]
            hh = num / jnp.maximum(jnp.abs(den), jnp.exp(-m_t))
            b_last = b_col[L - 1:L, :]
            m_new = m_t[L - 1:L, :]
            w_s = jnp.exp(b_last - b_col + i_col - m_new)
            cs = jnp.exp(b_last + m_prev - m_new)
            state_ref[h] = cs * st + _dot_tn((kh * w_s).astype(BF16), v_aug)
            m_ref[h:h + 1, :] = jnp.broadcast_to(m_new, (1, LANES))
            mean = jnp.mean(hh, axis=-1, keepdims=True)
            dv = hh - mean
            var = jnp.mean(dv * dv, axis=-1, keepdims=True)
            hn = dv * lax.rsqrt(var + MLSTM_LN_EPS) * lnw_ref[:, cols]
            out = _sigmoid(oo[rows, cols].astype(F32)) * hn + skip_ref[:, cols] * q
            y_ref[0, rows, cols] = (out * _silu(zz[rows, cols].astype(F32))).astype(y_ref.dtype)


def _mlstm(qk, v, o, z, g, p, tt, out_dtype):
    bsz, t, _ = qk.shape
    row = lambda b, i: (b, i, 0)
    c2 = lambda b, i: (0, 0)
    vec = lambda n: pl.BlockSpec((1, n), c2)
    v1 = lambda a: a.reshape(1, -1).astype(F32)
    gbias = jnp.concatenate([p["mlstm_b_i"], p["mlstm_b_f"],
                             jnp.zeros((GATE_PAD - 2 * MLSTM_HEADS,), F32)]).reshape(1, GATE_PAD)
    return pl.pallas_call(
        functools.partial(_mlstm_kernel, tt=tt),
        grid=(bsz, t // tt),
        in_specs=[pl.BlockSpec((1, tt, 2 * D_MLSTM), row),
                  pl.BlockSpec((1, tt, D_MLSTM), row),
                  pl.BlockSpec((1, tt, D_MLSTM), row),
                  pl.BlockSpec((1, tt, D_MLSTM), row),
                  pl.BlockSpec((1, tt, GATE_PAD), row),
                  pl.BlockSpec((CONV_K, 2 * D_MLSTM), c2),
                  vec(2 * D_MLSTM), vec(GATE_PAD), vec(D_MLSTM), vec(D_MLSTM),
                  pl.BlockSpec((tt, tt), c2)],
        out_specs=pl.BlockSpec((1, tt, D_MLSTM), row),
        out_shape=jax.ShapeDtypeStruct((bsz, t, D_MLSTM), out_dtype),
        scratch_shapes=[pltpu.VMEM((tt + 8, 2 * D_MLSTM), F32),
                        pltpu.VMEM((MLSTM_HEADS, MLSTM_HEAD, 2 * MLSTM_HEAD), F32),
                        pltpu.VMEM((8, LANES), F32)],
        compiler_params=pltpu.CompilerParams(dimension_semantics=("arbitrary", "arbitrary"),
                                             vmem_limit_bytes=VMEM_LIMIT),
        name="mlstm",
    )(qk, v, o, z, g, p["mlstm_conv_w"].astype(F32), v1(p["mlstm_conv_b"]), gbias,
      v1(p["mlstm_ln_w"]), v1(p["mlstm_skip"]), _chunk_tri(tt, MLSTM_CHUNK, "lower_incl"))


def _outproj_kernel(yr_ref, ym_ref, x_ref, gate_ref, w_ref, fg_ref, o_ref, *, final):
    mix = _dot(yr_ref[0], w_ref[0:D_RWKV, :]) + _dot(ym_ref[0], w_ref[D_RWKV:, :])
    h = x_ref[0] + gate_ref[0] * mix
    if final:
        ms = jnp.mean(h * h, axis=-1, keepdims=True)
        h = h * lax.rsqrt(ms + NORM_EPS) * fg_ref[...]
    o_ref[0] = h.astype(o_ref.dtype)


def _outproj(y_r, y_m, h_res, gate, w_out, final_gain, tm, final, out_dtype):
    bsz, t, d = h_res.shape
    row = lambda b, i: (b, i, 0)
    per_b = lambda b, i: (b, 0, 0)
    const = lambda b, i: (0, 0)
    return pl.pallas_call(
        functools.partial(_outproj_kernel, final=final),
        grid=(bsz, t // tm),
        in_specs=[pl.BlockSpec((1, tm, D_RWKV), row),
                  pl.BlockSpec((1, tm, D_MLSTM), row),
                  pl.BlockSpec((1, tm, d), row),
                  pl.BlockSpec((1, 1, d), per_b),
                  pl.BlockSpec((D_RWKV + D_MLSTM, d), const),
                  pl.BlockSpec((1, d), const)],
        out_specs=pl.BlockSpec((1, tm, d), row),
        out_shape=jax.ShapeDtypeStruct((bsz, t, d), out_dtype),
        compiler_params=pltpu.CompilerParams(dimension_semantics=("arbitrary", "arbitrary"),
                                             vmem_limit_bytes=VMEM_LIMIT),
        name="out_proj",
    )(y_r, y_m, h_res, gate.reshape(bsz, 1, d), w_out, final_gain.reshape(1, d))


def _permute_w_in(w):
    o = D_SHIFT
    z_r = w[:, o:o + D_RWKV]; o += D_RWKV
    q_m = w[:, o:o + D_MLSTM]; o += D_MLSTM
    k_m = w[:, o:o + D_MLSTM]; o += D_MLSTM
    v_m = w[:, o:o + D_MLSTM]; o += D_MLSTM
    o_m = w[:, o:o + D_MLSTM]; o += D_MLSTM
    i_m = w[:, o:o + MLSTM_HEADS]; o += MLSTM_HEADS
    f_m = w[:, o:o + MLSTM_HEADS]; o += MLSTM_HEADS
    z_m = w[:, o:o + D_MLSTM]
    pad = jnp.zeros((w.shape[0], GATE_PAD - 2 * MLSTM_HEADS), w.dtype)
    return jnp.concatenate([w[:, :D_SHIFT], z_r, q_m, k_m, v_m, o_m, z_m, i_m, f_m, pad],
                           axis=1).astype(BF16)


def _tile(t, want):
    tile = min(t, want)
    assert t % tile == 0
    return tile


def kernel(x, c, w_ada, b_ada, norm_gain, w_in, mu_rwkv, w_decay_up, w_decay0, w_icl_up, a0, k_k, k_a, r_k, rwkv_gn_w, rwkv_gn_b, mlstm_conv_w, mlstm_conv_b, mlstm_b_i, mlstm_b_f, mlstm_ln_w, mlstm_skip, w_out, final_gain):
    bsz, t, d = x.shape
    depth = w_ada.shape[0]
    act_dtype = F32
    tm = _tile(t, 256)
    tt_r = _tile(t, 256)
    tt_m = _tile(t, 256)
    assert t % RWKV_CHUNK == 0 and t % MLSTM_CHUNK == 0
    h_res = x.astype(F32)
    c32 = c.astype(F32)
    for l in range(depth):
        ada = _ada(c32, w_ada[l], b_ada[l])
        shift, scale, gate = ada[:, :d], ada[:, d:2 * d], ada[:, 2 * d:]
        u, z_r, qk_m, v_m, o_m, z_m, g_m = _inproj(h_res, scale, shift, norm_gain[l],
                                                   _permute_w_in(w_in[l]), tm, act_dtype)
        rp = dict(mu_rwkv=mu_rwkv[l], w_decay_up=w_decay_up[l], w_decay0=w_decay0[l],
                  w_icl_up=w_icl_up[l], a0=a0[l], k_k=k_k[l], k_a=k_a[l], r_k=r_k[l],
                  rwkv_gn_w=rwkv_gn_w[l], rwkv_gn_b=rwkv_gn_b[l])
        y_r = _rwkv(u, z_r, rp, tt_r, BF16)
        mp = dict(mlstm_conv_w=mlstm_conv_w[l], mlstm_conv_b=mlstm_conv_b[l], mlstm_b_i=mlstm_b_i[l],
                  mlstm_b_f=mlstm_b_f[l], mlstm_ln_w=mlstm_ln_w[l], mlstm_skip=mlstm_skip[l])
        y_m = _mlstm(qk_m, v_m, o_m, z_m, g_m, mp, tt_m, BF16)
        final = l == depth - 1
        h_res = _outproj(y_r, y_m, h_res, gate, w_out[l].astype(BF16), final_gain, tm, final,
                         x.dtype if final else F32)
    return h_res
```

```python
import functools

import jax
import jax.numpy as jnp
from jax import lax
from jax.experimental import pallas as pl
from jax.experimental.pallas import tpu as pltpu

F32 = jnp.float32
BF16 = jnp.bfloat16

RWKV_HEAD = 64
RWKV_HEADS = 8
D_RWKV = RWKV_HEAD * RWKV_HEADS
MLSTM_HEADS = 4
MLSTM_HEAD = 128
D_MLSTM = MLSTM_HEADS * MLSTM_HEAD
LORA = 64
D_SHIFT = 3 * D_RWKV + 2 * LORA
CONV_K = 4
NORM_EPS = 1e-6
RWKV_GN_EPS = 64e-5
MLSTM_LN_EPS = 1e-6
GATE_PAD = 128

RWKV_CHUNK = 64
MLSTM_CHUNK = 128
LANES = 128
VMEM_LIMIT = 56 * 1024 * 1024


def _dot(a, b):
    return jnp.dot(a, b, preferred_element_type=F32)


def _dot_nt(a, b):
    return lax.dot_general(a, b, (((1,), (1,)), ((), ())), preferred_element_type=F32)


def _dot_tn(a, b):
    return lax.dot_general(a, b, (((0,), (0,)), ((), ())), preferred_element_type=F32)


def _split_bf16(x, n):
    parts = []
    rem = x
    for i in range(n):
        p = rem.astype(BF16)
        parts.append(p)
        if i + 1 < n:
            rem = rem - p.astype(F32)
    return parts


def _dot_split_lhs(x, m, n):
    acc = None
    for p in _split_bf16(x, n):
        d = _dot(p, m)
        acc = d if acc is None else acc + d
    return acc


def _dot_split_rhs(m, x, n):
    acc = None
    for p in _split_bf16(x, n):
        d = _dot(m, p)
        acc = d if acc is None else acc + d
    return acc


def _sigmoid(x):
    return 1.0 / (1.0 + jnp.exp(-x))


def _silu(x):
    return x * _sigmoid(x)


def _ada_kernel(c_ref, w_ref, b_ref, o_ref):
    ca = _silu(c_ref[...])
    c_hi, c_lo = _split_bf16(ca, 2)
    w = w_ref[...]
    w_hi, w_lo = _split_bf16(w, 2)
    o_ref[...] = _dot(c_hi, w_hi) + _dot(c_hi, w_lo) + _dot(c_lo, w_hi) + b_ref[...]


def _ada(c, w, b):
    bsz, d = c.shape
    n = w.shape[1]
    bn = d
    return pl.pallas_call(
        _ada_kernel,
        grid=(n // bn,),
        in_specs=[pl.BlockSpec((bsz, d), lambda j: (0, 0)),
                  pl.BlockSpec((d, bn), lambda j: (0, j)),
                  pl.BlockSpec((1, bn), lambda j: (0, j))],
        out_specs=pl.BlockSpec((bsz, bn), lambda j: (0, j)),
        out_shape=jax.ShapeDtypeStruct((bsz, n), F32),
        compiler_params=pltpu.CompilerParams(dimension_semantics=("arbitrary",),
                                             vmem_limit_bytes=VMEM_LIMIT),
        name="ada",
    )(c, w, b.reshape(1, n))


IN_SPLITS = (D_SHIFT, D_RWKV, 2 * D_MLSTM, D_MLSTM, D_MLSTM, D_MLSTM, GATE_PAD)


def _inproj_kernel(x_ref, scale_ref, shift_ref, gain_ref, w_ref, *out_refs):
    x = x_ref[0]
    ms = jnp.mean(x * x, axis=-1, keepdims=True)
    hn = x * lax.rsqrt(ms + NORM_EPS) * gain_ref[...]
    hn = hn * (1.0 + scale_ref[0]) + shift_ref[0]
    hb = hn.astype(BF16)
    off = 0
    for o_ref, n in zip(out_refs, IN_SPLITS):
        o_ref[0] = _dot(hb, w_ref[:, off:off + n]).astype(o_ref.dtype)
        off += n


def _inproj(h_res, scale, shift, gain, w_perm, tm, act_dtype):
    bsz, t, d = h_res.shape
    n_tot = w_perm.shape[1]
    row = lambda b, i: (b, i, 0)
    per_b = lambda b, i: (b, 0, 0)
    const = lambda b, i: (0, 0)
    return pl.pallas_call(
        _inproj_kernel,
        grid=(bsz, t // tm),
        in_specs=[pl.BlockSpec((1, tm, d), row),
                  pl.BlockSpec((1, 1, d), per_b),
                  pl.BlockSpec((1, 1, d), per_b),
                  pl.BlockSpec((1, d), const),
                  pl.BlockSpec((d, n_tot), const, pipeline_mode=pl.Buffered(1))],
        out_specs=[pl.BlockSpec((1, tm, n), row) for n in IN_SPLITS],
        out_shape=[jax.ShapeDtypeStruct((bsz, t, n), act_dtype) for n in IN_SPLITS],
        compiler_params=pltpu.CompilerParams(dimension_semantics=("arbitrary", "arbitrary"),
                                             vmem_limit_bytes=VMEM_LIMIT),
        name="in_proj",
    )(h_res, scale.reshape(bsz, 1, d), shift.reshape(bsz, 1, d), gain.reshape(1, d), w_perm)


def _pair_blockdiag(x):
    lane = lax.broadcasted_iota(jnp.int32, x.shape, 1)
    first = lane < RWKV_HEAD
    zero = jnp.zeros_like(x)
    return jnp.concatenate([jnp.where(first, x, zero), jnp.where(first, zero, x)], axis=0)


def _rwkv_kernel(u_ref, z_ref, mu_ref, lora_ref, w0_ref, a0_ref, kk_ref, ka_ref, rk_ref,
                 gnw_ref, gnb_ref, hsum_ref, tril_ref,
                 y_ref,
                 carry_ref, state_ref, at_ref, rt_ref, bt_ref, kt_ref, bp_ref, kp_ref, vb_ref,
                 gam_ref, yacc_ref, *, tt):
    L = RWKV_CHUNK
    n_pairs = D_RWKV // LANES

    @pl.when(pl.program_id(1) == 0)
    def _():
        carry_ref[...] = jnp.zeros_like(carry_ref)
        state_ref[...] = jnp.zeros_like(state_ref)

    u_raw = u_ref[0].astype(F32)
    row = lax.broadcasted_iota(jnp.int32, u_raw.shape, 0)
    prev = jnp.where(row == 0, carry_ref[...], pltpu.roll(u_raw, 1, 0))
    carry_ref[...] = u_raw[tt - 1:tt, :]
    u = u_raw + mu_ref[...] * (prev - u_raw)

    r = u[:, 0:D_RWKV]
    k = u[:, D_RWKV:2 * D_RWKV]
    v = u[:, 2 * D_RWKV:3 * D_RWKV]
    g = u[:, 3 * D_RWKV:D_SHIFT]
    lw = _dot(jnp.tanh(g).astype(BF16), lora_ref[0])
    la = _dot(g.astype(BF16), lora_ref[1])

    ld = -jnp.exp(F32(-0.5)) * _sigmoid(w0_ref[...] + lw)
    a = _sigmoid(a0_ref[...] + la)
    hsum = hsum_ref[...]
    kk = k * kk_ref[...]
    ss = _dot_split_lhs(kk * kk, hsum, 1)
    kk = kk / jnp.maximum(jnp.sqrt(ss), 1e-12)
    k2 = k * (1.0 + (a - 1.0) * ka_ref[...])
    bonus = _dot_split_lhs(r * k2 * rk_ref[...], hsum, 1) * v
    kka = kk * a

    c = _dot_split_rhs(tril_ref[...], ld, 2)
    gam_ref[...] = c
    c_last = jnp.concatenate(
        [jnp.broadcast_to(gam_ref[ci * L + L - 1:ci * L + L, :], (L, D_RWKV)) for ci in range(tt // L)],
        axis=0)
    rev = c_last - c
    e_pos = jnp.exp(c)
    e_neg = jnp.exp(-c)
    e_rev = jnp.exp(rev)
    rt_ref[...] = (r * e_pos).astype(BF16)
    kt_ref[...] = (k2 * e_neg).astype(BF16)
    at_ref[...] = (-kk * jnp.exp(c - ld)).astype(BF16)
    bt_ref[...] = (kka * e_neg).astype(BF16)
    kp_ref[...] = (k2 * e_rev).astype(BF16)
    bp_ref[...] = (kka * e_rev).astype(BF16)
    vb_ref[...] = v.astype(BF16)
    gam_ref[...] = jnp.exp(c_last)

    ti = lax.broadcasted_iota(jnp.int32, (L, LANES), 0)
    si = lax.broadcasted_iota(jnp.int32, (L, LANES), 1) % L
    strict = si < ti
    incl = si <= ti
    eye_pair = (si == ti).astype(F32)
    xr = ti ^ si
    r2 = lax.broadcasted_iota(jnp.int32, (LANES, LANES), 0)
    c2 = lax.broadcasted_iota(jnp.int32, (LANES, LANES), 1)
    same_head = (r2 < RWKV_HEAD) == (c2 < RWKV_HEAD)
    eye_full = r2 == c2

    n_chunks = tt // L
    chains = [(ci, p) for p in range(n_pairs) for ci in range(n_chunks)]

    def load(ref):
        return [ref[ci * L:(ci + 1) * L, p * LANES:(p + 1) * LANES] for ci, p in chains]

    at, rt, bt, kt, bp, kp, vb = (load(ref) for ref in
                                  (at_ref, rt_ref, bt_ref, kt_ref, bp_ref, kp_ref, vb_ref))
    zero = jnp.zeros((L, LANES), F32)
    zero2 = jnp.zeros((LANES, LANES), F32)
    aa = [_dot_nt(jnp.concatenate([a_, r_], axis=0),
                  jnp.concatenate([_pair_blockdiag(b_), _pair_blockdiag(k_)], axis=0))
          for a_, r_, b_, k_ in zip(at, rt, bt, kt)]
    ab = [jnp.where(strict, x[:L, :LANES], zero) for x in aa]
    ak = [jnp.where(strict, x[:L, LANES:], zero).astype(BF16) for x in aa]
    rb = [jnp.where(incl, x[L:, :LANES], zero).astype(BF16) for x in aa]
    rk = [jnp.where(incl, x[L:, LANES:], zero).astype(BF16) for x in aa]
    tinv = [eye_pair + jnp.where((xr >> 1) == 0, x, zero) for x in ab]
    for lvl in range(1, 6):
        yy = [_dot(jnp.where((xr >> lvl) == 1, x, zero).astype(BF16), _pair_blockdiag(t.astype(BF16)))
              for x, t in zip(ab, tinv)]
        tinv = [t + _dot(t.astype(BF16), _pair_blockdiag(y.astype(BF16))) for t, y in zip(tinv, yy)]
    vbd = [_pair_blockdiag(x) for x in vb]
    w = [_dot(x, y).astype(BF16) for x, y in zip(ak, vbd)]
    ap = [_dot(t.astype(BF16), jnp.concatenate([_pair_blockdiag(a_), _pair_blockdiag(w_)], axis=1))
          for t, a_, w_ in zip(tinv, at, w)]
    ahat = [x[:, :LANES].astype(BF16) for x in ap]
    p1 = [x[:, LANES:].astype(BF16) for x in ap]
    q = [_dot(x, jnp.concatenate([_pair_blockdiag(a_), _pair_blockdiag(p_)], axis=1))
         for x, a_, p_ in zip(rb, ahat, p1)]
    y_in = [q_[:, LANES:] + _dot(x, y) for q_, x, y in zip(q, rk, vbd)]
    mg = [_dot_tn(jnp.concatenate([b_, k_], axis=0),
                  jnp.concatenate([jnp.concatenate([a_, p_], axis=1),
                                   jnp.concatenate([jnp.zeros_like(v_), v_], axis=1)], axis=0))
          for b_, k_, a_, p_, v_ in zip(bp, kp, ahat, p1, vb)]
    lhs = [jnp.concatenate([r_.astype(F32) + q_[:, :LANES], jnp.where(same_head, m_[:, :LANES], zero2)],
                           axis=0).astype(BF16) for r_, q_, m_ in zip(rt, q, mg)]
    gg = [jnp.where(same_head, m_[:, LANES:], zero2) for m_ in mg]
    gam_col = [jnp.sum(jnp.where(eye_full,
                                 jnp.broadcast_to(gam_ref[ci * L:ci * L + 1, p * LANES:(p + 1) * LANES],
                                                  (LANES, LANES)), zero2), axis=1, keepdims=True)
               for ci, p in chains]

    hs = [state_ref[p] for p in range(n_pairs)]
    for ci in range(n_chunks):
        idx = [p * n_chunks + ci for p in range(n_pairs)]
        yh = [_dot(lhs[i], h.astype(BF16)) for i, h in zip(idx, hs)]
        for p, i in enumerate(idx):
            yacc_ref[ci * L:(ci + 1) * L, p * LANES:(p + 1) * LANES] = yh[p][:L] + y_in[i]
        hs = [gam_col[i] * h + y_[L:] + gg[i] for i, h, y_ in zip(idx, hs, yh)]
    for p in range(n_pairs):
        state_ref[p] = hs[p]

    y = yacc_ref[...]
    inv_n = F32(1.0 / RWKV_HEAD)
    mean = _dot_split_lhs(y, hsum, 1) * inv_n
    d = y - mean
    var = _dot_split_lhs(d * d, hsum, 1) * inv_n
    yn = d * lax.rsqrt(var + RWKV_GN_EPS) * gnw_ref[...] + gnb_ref[...]
    y_ref[0] = ((yn + bonus) * _silu(z_ref[0].astype(F32))).astype(y_ref.dtype)


def _chunk_tri(n, chunk):
    i = jnp.arange(n)[:, None]
    j = jnp.arange(n)[None, :]
    return (((i // chunk) == (j // chunk)) & (j <= i)).astype(BF16)


def _rwkv(u, z, p, tt, out_dtype):
    bsz, t, _ = u.shape
    row = lambda b, i: (b, i, 0)
    c2 = lambda b, i: (0, 0)
    c3 = lambda b, i: (0, 0, 0)
    vec = lambda n: pl.BlockSpec((1, n), c2)
    hsum = ((jnp.arange(D_RWKV)[:, None] // RWKV_HEAD) ==
            (jnp.arange(D_RWKV)[None, :] // RWKV_HEAD)).astype(BF16)
    lora = jnp.stack([
        jnp.concatenate([p["w_decay_up"], jnp.zeros((LORA, D_RWKV), F32)], axis=0),
        jnp.concatenate([jnp.zeros((LORA, D_RWKV), F32), p["w_icl_up"]], axis=0)]).astype(BF16)
    v1 = lambda a: a.reshape(1, -1).astype(F32)
    act = lambda: pltpu.VMEM((tt, D_RWKV), BF16)
    return pl.pallas_call(
        functools.partial(_rwkv_kernel, tt=tt),
        grid=(bsz, t // tt),
        in_specs=[pl.BlockSpec((1, tt, D_SHIFT), row),
                  pl.BlockSpec((1, tt, D_RWKV), row),
                  vec(D_SHIFT),
                  pl.BlockSpec((2, 2 * LORA, D_RWKV), c3),
                  vec(D_RWKV), vec(D_RWKV), vec(D_RWKV), vec(D_RWKV), vec(D_RWKV),
                  vec(D_RWKV), vec(D_RWKV),
                  pl.BlockSpec((D_RWKV, D_RWKV), c2),
                  pl.BlockSpec((tt, tt), c2)],
        out_specs=pl.BlockSpec((1, tt, D_RWKV), row),
        out_shape=jax.ShapeDtypeStruct((bsz, t, D_RWKV), out_dtype),
        scratch_shapes=[pltpu.VMEM((1, D_SHIFT), F32),
                        pltpu.VMEM((D_RWKV // LANES, LANES, LANES), F32),
                        act(), act(), act(), act(), act(), act(), act(),
                        pltpu.VMEM((tt, D_RWKV), F32),
                        pltpu.VMEM((tt, D_RWKV), F32)],
        compiler_params=pltpu.CompilerParams(dimension_semantics=("arbitrary", "arbitrary"),
                                             vmem_limit_bytes=VMEM_LIMIT),
        name="rwkv7",
    )(u, z, v1(p["mu_rwkv"]), lora, v1(p["w_decay0"]), v1(p["a0"]), v1(p["k_k"]), v1(p["k_a"]),
      v1(p["r_k"]), v1(p["rwkv_gn_w"]), v1(p["rwkv_gn_b"]), hsum,
      _chunk_tri(tt, RWKV_CHUNK))


def _mlstm_kernel(qk_ref, v_ref, o_ref, z_ref, g_ref, convw_ref, convb_ref, gbias_ref, lnw_ref,
                  skip_ref, tril_ref,
                  y_ref,
                  xbuf_ref, state_ref, m_ref, *, tt):
    L = MLSTM_CHUNK
    H = MLSTM_HEADS
    dh = MLSTM_HEAD

    @pl.when(pl.program_id(1) == 0)
    def _():
        xbuf_ref[0:8, :] = jnp.zeros((8, 2 * D_MLSTM), F32)
        state_ref[...] = jnp.zeros_like(state_ref)
        m_ref[...] = jnp.zeros_like(m_ref)

    xbuf_ref[8:8 + tt, :] = qk_ref[0].astype(F32)
    xb = xbuf_ref[...]
    acc = convb_ref[...] + convw_ref[CONV_K - 1:CONV_K, :] * xb[8:8 + tt]
    for j in range(CONV_K - 1):
        sh = CONV_K - 1 - j
        acc = acc + convw_ref[j:j + 1, :] * pltpu.roll(xb, sh, 0)[8:8 + tt]
    xbuf_ref[0:8, :] = xb[tt:tt + 8]
    qk = _silu(acc)
    q_c = qk[:, :D_MLSTM]
    k_c = qk[:, D_MLSTM:] * F32(dh ** -0.5)

    gt = g_ref[0].astype(F32) + gbias_ref[...]
    lane = lax.broadcasted_iota(jnp.int32, gt.shape, 1)
    is_f = (lane >= H) & (lane < 2 * H)
    log_f = jnp.minimum(gt, 0.0) - jnp.log(1.0 + jnp.exp(-jnp.abs(gt)))
    gates = jnp.where(is_f, log_f, gt)
    bcum = _dot_split_rhs(tril_ref[...], gates, 2)

    ti = lax.broadcasted_iota(jnp.int32, (L, L), 0)
    si = lax.broadcasted_iota(jnp.int32, (L, L), 1)
    causal = si <= ti
    ones_l = jnp.ones((L, dh), BF16)
    n_chunks = tt // L
    units = [(ci, h) for ci in range(n_chunks) for h in range(H)]
    rows_of = lambda ci: slice(ci * L, (ci + 1) * L)
    cols_of = lambda h: slice(h * dh, (h + 1) * dh)

    g_rows = [jnp.transpose(gates[rows_of(ci)]) for ci in range(n_chunks)]
    b_rows = [jnp.transpose(bcum[rows_of(ci)]) for ci in range(n_chunks)]
    q = [q_c[rows_of(ci), cols_of(h)].astype(BF16) for ci, h in units]
    kh = [k_c[rows_of(ci), cols_of(h)] for ci, h in units]
    v_aug = [jnp.concatenate([v_ref[0, rows_of(ci), cols_of(h)].astype(BF16), ones_l], axis=1)
             for ci, h in units]
    b_col = [bcum[rows_of(ci), H + h:H + h + 1] for ci, h in units]
    i_col = [gates[rows_of(ci), h:h + 1] for ci, h in units]
    row_v = [g_rows[ci][h:h + 1, :] - b_rows[ci][H + h:H + h + 1, :] for ci, h in units]
    d_log = [jnp.where(causal, bc + rv, -jnp.inf) for bc, rv in zip(b_col, row_v)]
    a_t = [jnp.max(d, axis=-1, keepdims=True) for d in d_log]
    scores = [(_dot_nt(q_, k_.astype(BF16)) * jnp.exp(d - a)).astype(BF16)
              for q_, k_, d, a in zip(q, kh, d_log, a_t)]
    nd_loc = [_dot(s, va) for s, va in zip(scores, v_aug)]
    b_last = [bc[L - 1:L, :] for bc in b_col]
    a_last = [a[L - 1:L, :] for a in a_t]
    kw = [(k_ * jnp.exp(bl - bc + ic - al)).astype(BF16)
          for k_, bl, bc, ic, al in zip(kh, b_last, b_col, i_col, a_last)]
    upd = [_dot_tn(kw_, va) for kw_, va in zip(kw, v_aug)]

    st = [state_ref[h] for h in range(H)]
    m_prev = [m_ref[h:h + 1, 0:1] for h in range(H)]
    for ci in range(n_chunks):
        ids = [ci * H + h for h in range(H)]
        inter_log = [b_col[i] + m_prev[h] for h, i in enumerate(ids)]
        m_t = [jnp.maximum(il, a_t[i]) for il, i in zip(inter_log, ids)]
        qs = [_dot(q[i], st[h].astype(BF16)) for h, i in enumerate(ids)]
        nd = [jnp.exp(a_t[i] - mt) * nd_loc[i] + jnp.exp(il - mt) * qs_
              for i, mt, il, qs_ in zip(ids, m_t, inter_log, qs)]
        hh = [x[:, :dh] / jnp.maximum(jnp.abs(x[:, dh:]), jnp.exp(-mt)) for x, mt in zip(nd, m_t)]
        m_new = [mt[L - 1:L, :] for mt in m_t]
        st = [jnp.exp(b_last[i] + mp - mn) * s_ + jnp.exp(a_last[i] - mn) * upd[i]
              for i, mp, mn, s_ in zip(ids, m_prev, m_new, st)]
        m_prev = m_new
        mean = [jnp.mean(x, axis=-1, keepdims=True) for x in hh]
        dv = [x - mu for x, mu in zip(hh, mean)]
        var = [jnp.mean(x * x, axis=-1, keepdims=True) for x in dv]
        for h in range(H):
            rows, cols = rows_of(ci), cols_of(h)
            hn = dv[h] * lax.rsqrt(var[h] + MLSTM_LN_EPS) * lnw_ref[:, cols]
            out = _sigmoid(o_ref[0, rows, cols].astype(F32)) * hn + skip_ref[:, cols] * q_c[rows, cols]
            y_ref[0, rows, cols] = (out * _silu(z_ref[0, rows, cols].astype(F32))).astype(y_ref.dtype)
    for h in range(H):
        state_ref[h] = st[h]
        m_ref[h:h + 1, :] = jnp.broadcast_to(m_prev[h], (1, LANES))


def _mlstm(qk, v, o, z, g, p, tt, out_dtype):
    bsz, t, _ = qk.shape
    row = lambda b, i: (b, i, 0)
    c2 = lambda b, i: (0, 0)
    vec = lambda n: pl.BlockSpec((1, n), c2)
    v1 = lambda a: a.reshape(1, -1).astype(F32)
    gbias = jnp.concatenate([p["mlstm_b_i"], p["mlstm_b_f"],
                             jnp.zeros((GATE_PAD - 2 * MLSTM_HEADS,), F32)]).reshape(1, GATE_PAD)
    return pl.pallas_call(
        functools.partial(_mlstm_kernel, tt=tt),
        grid=(bsz, t // tt),
        in_specs=[pl.BlockSpec((1, tt, 2 * D_MLSTM), row),
                  pl.BlockSpec((1, tt, D_MLSTM), row),
                  pl.BlockSpec((1, tt, D_MLSTM), row),
                  pl.BlockSpec((1, tt, D_MLSTM), row),
                  pl.BlockSpec((1, tt, GATE_PAD), row),
                  pl.BlockSpec((CONV_K, 2 * D_MLSTM), c2),
                  vec(2 * D_MLSTM), vec(GATE_PAD), vec(D_MLSTM), vec(D_MLSTM),
                  pl.BlockSpec((tt, tt), c2)],
        out_specs=pl.BlockSpec((1, tt, D_MLSTM), row),
        out_shape=jax.ShapeDtypeStruct((bsz, t, D_MLSTM), out_dtype),
        scratch_shapes=[pltpu.VMEM((tt + 8, 2 * D_MLSTM), F32),
                        pltpu.VMEM((MLSTM_HEADS, MLSTM_HEAD, 2 * MLSTM_HEAD), F32),
                        pltpu.VMEM((8, LANES), F32)],
        compiler_params=pltpu.CompilerParams(dimension_semantics=("arbitrary", "arbitrary"),
                                             vmem_limit_bytes=VMEM_LIMIT),
        name="mlstm",
    )(qk, v, o, z, g, p["mlstm_conv_w"].astype(F32), v1(p["mlstm_conv_b"]), gbias,
      v1(p["mlstm_ln_w"]), v1(p["mlstm_skip"]), _chunk_tri(tt, MLSTM_CHUNK))


def _outproj_kernel(yr_ref, ym_ref, x_ref, gate_ref, w_ref, fg_ref, o_ref, *, final):
    mix = _dot(yr_ref[0], w_ref[0:D_RWKV, :]) + _dot(ym_ref[0], w_ref[D_RWKV:, :])
    h = x_ref[0] + gate_ref[0] * mix
    if final:
        ms = jnp.mean(h * h, axis=-1, keepdims=True)
        h = h * lax.rsqrt(ms + NORM_EPS) * fg_ref[...]
    o_ref[0] = h.astype(o_ref.dtype)


def _outproj(y_r, y_m, h_res, gate, w_out, final_gain, tm, final, out_dtype):
    bsz, t, d = h_res.shape
    row = lambda b, i: (b, i, 0)
    per_b = lambda b, i: (b, 0, 0)
    const = lambda b, i: (0, 0)
    return pl.pallas_call(
        functools.partial(_outproj_kernel, final=final),
        grid=(bsz, t // tm),
        in_specs=[pl.BlockSpec((1, tm, D_RWKV), row),
                  pl.BlockSpec((1, tm, D_MLSTM), row),
                  pl.BlockSpec((1, tm, d), row),
                  pl.BlockSpec((1, 1, d), per_b),
                  pl.BlockSpec((D_RWKV + D_MLSTM, d), const, pipeline_mode=pl.Buffered(1)),
                  pl.BlockSpec((1, d), const)],
        out_specs=pl.BlockSpec((1, tm, d), row),
        out_shape=jax.ShapeDtypeStruct((bsz, t, d), out_dtype),
        compiler_params=pltpu.CompilerParams(dimension_semantics=("arbitrary", "arbitrary"),
                                             vmem_limit_bytes=VMEM_LIMIT),
        name="out_proj",
    )(y_r, y_m, h_res, gate.reshape(bsz, 1, d), w_out, final_gain.reshape(1, d))


def _permute_w_in(w):
    o = D_SHIFT
    z_r = w[:, o:o + D_RWKV]; o += D_RWKV
    q_m = w[:, o:o + D_MLSTM]; o += D_MLSTM
    k_m = w[:, o:o + D_MLSTM]; o += D_MLSTM
    v_m = w[:, o:o + D_MLSTM]; o += D_MLSTM
    o_m = w[:, o:o + D_MLSTM]; o += D_MLSTM
    i_m = w[:, o:o + MLSTM_HEADS]; o += MLSTM_HEADS
    f_m = w[:, o:o + MLSTM_HEADS]; o += MLSTM_HEADS
    z_m = w[:, o:o + D_MLSTM]
    pad = jnp.zeros((w.shape[0], GATE_PAD - 2 * MLSTM_HEADS), w.dtype)
    return jnp.concatenate([w[:, :D_SHIFT], z_r, q_m, k_m, v_m, o_m, z_m, i_m, f_m, pad],
                           axis=1).astype(BF16)


def _tile(t, want):
    tile = min(t, want)
    assert t % tile == 0
    return tile


def kernel(x, c, w_ada, b_ada, norm_gain, w_in, mu_rwkv, w_decay_up, w_decay0, w_icl_up, a0, k_k, k_a, r_k, rwkv_gn_w, rwkv_gn_b, mlstm_conv_w, mlstm_conv_b, mlstm_b_i, mlstm_b_f, mlstm_ln_w, mlstm_skip, w_out, final_gain):
    bsz, t, d = x.shape
    depth = w_ada.shape[0]
    act_dtype = BF16
    tm = _tile(t, 512)
    tt_r = _tile(t, 256)
    tt_m = _tile(t, 256)
    assert t % RWKV_CHUNK == 0 and t % MLSTM_CHUNK == 0
    h_res = x.astype(F32)
    c32 = c.astype(F32)
    for l in range(depth):
        ada = _ada(c32, w_ada[l], b_ada[l])
        shift, scale, gate = ada[:, :d], ada[:, d:2 * d], ada[:, 2 * d:]
        u, z_r, qk_m, v_m, o_m, z_m, g_m = _inproj(h_res, scale, shift, norm_gain[l],
                                                   _permute_w_in(w_in[l]), tm, act_dtype)
        rp = dict(mu_rwkv=mu_rwkv[l], w_decay_up=w_decay_up[l], w_decay0=w_decay0[l],
                  w_icl_up=w_icl_up[l], a0=a0[l], k_k=k_k[l], k_a=k_a[l], r_k=r_k[l],
                  rwkv_gn_w=rwkv_gn_w[l], rwkv_gn_b=rwkv_gn_b[l])
        y_r = _rwkv(u, z_r, rp, tt_r, BF16)
        mp = dict(mlstm_conv_w=mlstm_conv_w[l], mlstm_conv_b=mlstm_conv_b[l], mlstm_b_i=mlstm_b_i[l],
                  mlstm_b_f=mlstm_b_f[l], mlstm_ln_w=mlstm_ln_w[l], mlstm_skip=mlstm_skip[l])
        y_m = _mlstm(qk_m, v_m, o_m, z_m, g_m, mp, tt_m, BF16)
        final = l == depth - 1
        h_res = _outproj(y_r, y_m, h_res, gate, w_out[l].astype(BF16), final_gain, tm, final,
                         x.dtype if final else F32)
    return h_res
```

```python
import functools

import jax
import jax.numpy as jnp
from jax import lax
from jax.experimental import pallas as pl
from jax.experimental.pallas import tpu as pltpu

F32 = jnp.float32
BF16 = jnp.bfloat16

RWKV_HEAD = 64
RWKV_HEADS = 8
D_RWKV = RWKV_HEAD * RWKV_HEADS
MLSTM_HEADS = 4
MLSTM_HEAD = 128
D_MLSTM = MLSTM_HEADS * MLSTM_HEAD
LORA = 64
D_SHIFT = 3 * D_RWKV + 2 * LORA
CONV_K = 4
NORM_EPS = 1e-6
RWKV_GN_EPS = 64e-5
MLSTM_LN_EPS = 1e-6
GATE_PAD = 128

RWKV_CHUNK = 64
MLSTM_CHUNK = 128
LANES = 128
VMEM_LIMIT = 56 * 1024 * 1024


def _dot(a, b):
    return jnp.dot(a, b, preferred_element_type=F32)


def _dot_nt(a, b):
    return lax.dot_general(a, b, (((1,), (1,)), ((), ())), preferred_element_type=F32)


def _dot_tn(a, b):
    return lax.dot_general(a, b, (((0,), (0,)), ((), ())), preferred_element_type=F32)


def _split_bf16(x, n):
    parts = []
    rem = x
    for i in range(n):
        p = rem.astype(BF16)
        parts.append(p)
        if i + 1 < n:
            rem = rem - p.astype(F32)
    return parts


def _dot_split_rhs(m, x, n):
    acc = None
    for p in _split_bf16(x, n):
        d = _dot(m, p)
        acc = d if acc is None else acc + d
    return acc


def _sigmoid(x):
    return 1.0 / (1.0 + jnp.exp(-x))


def _silu(x):
    return x * _sigmoid(x)


def _ada_kernel(c_ref, w_ref, b_ref, o_ref):
    ca = _silu(c_ref[...])
    c_hi, c_lo = _split_bf16(ca, 2)
    w = w_ref[...]
    w_hi, w_lo = _split_bf16(w, 2)
    o_ref[...] = _dot(c_hi, w_hi) + _dot(c_hi, w_lo) + _dot(c_lo, w_hi) + b_ref[...]


def _ada(c, w, b):
    bsz, d = c.shape
    n = w.shape[1]
    bn = d
    return pl.pallas_call(
        _ada_kernel,
        grid=(n // bn,),
        in_specs=[pl.BlockSpec((bsz, d), lambda j: (0, 0)),
                  pl.BlockSpec((d, bn), lambda j: (0, j)),
                  pl.BlockSpec((1, bn), lambda j: (0, j))],
        out_specs=pl.BlockSpec((bsz, bn), lambda j: (0, j)),
        out_shape=jax.ShapeDtypeStruct((bsz, n), F32),
        compiler_params=pltpu.CompilerParams(dimension_semantics=("arbitrary",),
                                             vmem_limit_bytes=VMEM_LIMIT),
        name="ada",
    )(c, w, b.reshape(1, n))


RWKV_COLS = D_SHIFT + D_RWKV


def _adaln_norm(x, scale, shift, gain):
    ms = jnp.mean(x * x, axis=-1, keepdims=True)
    return x * lax.rsqrt(ms + NORM_EPS) * gain * (1.0 + scale) + shift


def _pair_blockdiag(x):
    lane = lax.broadcasted_iota(jnp.int32, x.shape, 1)
    first = lane < RWKV_HEAD
    zero = jnp.zeros_like(x)
    return jnp.concatenate([jnp.where(first, x, zero), jnp.where(first, zero, x)], axis=0)


def _rwkv_kernel(x_ref, scale_ref, shift_ref, gain_ref, w_ref, mu_ref, lora_ref, w0_ref, a0_ref,
                 kk_ref, ka_ref, rk_ref, gnw_ref, gnb_ref, hsum_ref, tril_ref,
                 y_ref,
                 proj_a, proj_b, carry_ref, state_ref, at_ref, rt_ref, bt_ref, kt_ref, bp_ref, kp_ref,
                 vb_ref, gam_ref, yacc_ref, bonus_ref, *, tt, n_t):
    s = pl.program_id(0)

    @pl.when(s == 0)
    def _():
        proj_b[...] = jnp.zeros_like(proj_b)

    @pl.when(jnp.logical_or(s == 0, lax.rem(s + n_t - 1, n_t) == 0))
    def _():
        carry_ref[...] = jnp.zeros_like(carry_ref)
        state_ref[...] = jnp.zeros_like(state_ref)

    refs = dict(x=x_ref, scale=scale_ref, shift=shift_ref, gain=gain_ref, w=w_ref, mu=mu_ref,
                lora=lora_ref, w0=w0_ref, a0=a0_ref, kk=kk_ref, ka=ka_ref, rk=rk_ref, gnw=gnw_ref,
                gnb=gnb_ref, hsum=hsum_ref, tril=tril_ref, y=y_ref, carry=carry_ref, state=state_ref,
                at=at_ref, rt=rt_ref, bt=bt_ref, kt=kt_ref, bp=bp_ref, kp=kp_ref, vb=vb_ref,
                gam=gam_ref, yacc=yacc_ref, bonus=bonus_ref)

    @pl.when(s % 2 == 0)
    def _():
        _rwkv_step(refs, proj_b, proj_a, tt)

    @pl.when(s % 2 == 1)
    def _():
        _rwkv_step(refs, proj_a, proj_b, tt)


def _rwkv_step(R, src_ref, dst_ref, tt):
    L = RWKV_CHUNK
    n_pairs = D_RWKV // LANES
    n_chunks = tt // L

    hn = _adaln_norm(R["x"][0], R["scale"][0], R["shift"][0], R["gain"][...]).astype(BF16)
    pending = [(o, min(2 * LANES, RWKV_COLS - o)) for o in range(0, RWKV_COLS, 2 * LANES)]

    def tick(n=1):
        for _ in range(n):
            if pending:
                o, w = pending.pop(0)
                dst_ref[:, o:o + w] = _dot(hn, R["w"][:, o:o + w])

    row0 = lax.broadcasted_iota(jnp.int32, (tt, LANES), 0) == 0

    def shifted(g):
        cols = slice(g * LANES, (g + 1) * LANES)
        raw = src_ref[:, cols]
        prev = jnp.where(row0, R["carry"][:, cols], pltpu.roll(raw, 1, 0))
        R["carry"][:, cols] = raw[tt - 1:tt, :]
        return raw + R["mu"][:, cols] * (prev - raw)

    g_lora = shifted(3 * n_pairs)
    lw = _dot(jnp.tanh(g_lora).astype(BF16), R["lora"][0])
    la = _dot(g_lora.astype(BF16), R["lora"][1])
    hsum = R["hsum"][...]
    tick()
    for p in range(n_pairs):
        cols = slice(p * LANES, (p + 1) * LANES)
        r = shifted(p)
        k = shifted(n_pairs + p)
        v = shifted(2 * n_pairs + p)
        ld = -jnp.exp(F32(-0.5)) * _sigmoid(R["w0"][:, cols] + lw[:, cols])
        a = _sigmoid(R["a0"][:, cols] + la[:, cols])
        kk = k * R["kk"][:, cols]
        ss = _dot((kk * kk).astype(BF16), hsum)
        kk = kk / jnp.maximum(jnp.sqrt(ss), 1e-12)
        k2 = k * (1.0 + (a - 1.0) * R["ka"][:, cols])
        R["bonus"][:, cols] = _dot((r * k2 * R["rk"][:, cols]).astype(BF16), hsum) * v
        kka = kk * a
        c = _dot_split_rhs(R["tril"][...], ld, 2)
        R["gam"][:, cols] = c
        c_last = jnp.concatenate(
            [jnp.broadcast_to(R["gam"][ci * L + L - 1:ci * L + L, cols], (L, LANES))
             for ci in range(n_chunks)], axis=0)
        e_pos = jnp.exp(c)
        e_neg = jnp.exp(-c)
        e_rev = jnp.exp(c_last - c)
        R["rt"][:, cols] = (r * e_pos).astype(BF16)
        R["kt"][:, cols] = (k2 * e_neg).astype(BF16)
        R["at"][:, cols] = (-kk * jnp.exp(c - ld)).astype(BF16)
        R["bt"][:, cols] = (kka * e_neg).astype(BF16)
        R["kp"][:, cols] = (k2 * e_rev).astype(BF16)
        R["bp"][:, cols] = (kka * e_rev).astype(BF16)
        R["vb"][:, cols] = v.astype(BF16)
        R["gam"][:, cols] = jnp.exp(c_last)
        tick(2)

    ti = lax.broadcasted_iota(jnp.int32, (L, LANES), 0)
    si = lax.broadcasted_iota(jnp.int32, (L, LANES), 1) % L
    strict = si < ti
    incl = si <= ti
    eye_pair = (si == ti).astype(F32)
    xr = ti ^ si
    r2 = lax.broadcasted_iota(jnp.int32, (LANES, LANES), 0)
    c2 = lax.broadcasted_iota(jnp.int32, (LANES, LANES), 1)
    same_head = (r2 < RWKV_HEAD) == (c2 < RWKV_HEAD)
    eye_full = r2 == c2

    chains = [(ci, p) for p in range(n_pairs) for ci in range(n_chunks)]

    def load(name):
        return [R[name][ci * L:(ci + 1) * L, p * LANES:(p + 1) * LANES] for ci, p in chains]

    at, rt, bt, kt, bp, kp, vb = (load(n) for n in ("at", "rt", "bt", "kt", "bp", "kp", "vb"))
    zero = jnp.zeros((L, LANES), F32)
    zero2 = jnp.zeros((LANES, LANES), F32)
    aa = [_dot_nt(jnp.concatenate([a_, r_], axis=0),
                  jnp.concatenate([_pair_blockdiag(b_), _pair_blockdiag(k_)], axis=0))
          for a_, r_, b_, k_ in zip(at, rt, bt, kt)]
    ab = [jnp.where(strict, x[:L, :LANES], zero) for x in aa]
    ak = [jnp.where(strict, x[:L, LANES:], zero).astype(BF16) for x in aa]
    rb = [jnp.where(incl, x[L:, :LANES], zero).astype(BF16) for x in aa]
    rk = [jnp.where(incl, x[L:, LANES:], zero).astype(BF16) for x in aa]
    tinv = [eye_pair + jnp.where((xr >> 1) == 0, x, zero) for x in ab]
    for lvl in range(1, 6):
        yy = [_dot(jnp.where((xr >> lvl) == 1, x, zero).astype(BF16), _pair_blockdiag(t.astype(BF16)))
              for x, t in zip(ab, tinv)]
        tinv = [t + _dot(t.astype(BF16), _pair_blockdiag(y.astype(BF16))) for t, y in zip(tinv, yy)]
    vbd = [_pair_blockdiag(x) for x in vb]
    w = [_dot(x, y).astype(BF16) for x, y in zip(ak, vbd)]
    ap = [_dot(t.astype(BF16), jnp.concatenate([_pair_blockdiag(a_), _pair_blockdiag(w_)], axis=1))
          for t, a_, w_ in zip(tinv, at, w)]
    ahat = [x[:, :LANES].astype(BF16) for x in ap]
    p1 = [x[:, LANES:].astype(BF16) for x in ap]
    q = [_dot(x, jnp.concatenate([_pair_blockdiag(a_), _pair_blockdiag(p_)], axis=1))
         for x, a_, p_ in zip(rb, ahat, p1)]
    y_in = [q_[:, LANES:] + _dot(x, y) for q_, x, y in zip(q, rk, vbd)]
    mg = [_dot_tn(jnp.concatenate([b_, k_], axis=0),
                  jnp.concatenate([jnp.concatenate([a_, p_], axis=1),
                                   jnp.concatenate([jnp.zeros_like(v_), v_], axis=1)], axis=0))
          for b_, k_, a_, p_, v_ in zip(bp, kp, ahat, p1, vb)]
    lhs = [jnp.concatenate([r_.astype(F32) + q_[:, :LANES], jnp.where(same_head, m_[:, :LANES], zero2)],
                           axis=0).astype(BF16) for r_, q_, m_ in zip(rt, q, mg)]
    gg = [jnp.where(same_head, m_[:, LANES:], zero2) for m_ in mg]
    gam_col = [jnp.sum(jnp.where(eye_full,
                                 jnp.broadcast_to(R["gam"][ci * L:ci * L + 1, p * LANES:(p + 1) * LANES],
                                                  (LANES, LANES)), zero2), axis=1, keepdims=True)
               for ci, p in chains]

    hs = [R["state"][p] for p in range(n_pairs)]
    for ci in range(n_chunks):
        idx = [p * n_chunks + ci for p in range(n_pairs)]
        yh = [_dot(lhs[i], h.astype(BF16)) for i, h in zip(idx, hs)]
        for p, i in enumerate(idx):
            R["yacc"][ci * L:(ci + 1) * L, p * LANES:(p + 1) * LANES] = yh[p][:L] + y_in[i]
        hs = [gam_col[i] * h + y_[L:] + gg[i] for i, h, y_ in zip(idx, hs, yh)]
    for p in range(n_pairs):
        R["state"][p] = hs[p]

    inv_n = F32(1.0 / RWKV_HEAD)
    for p in range(n_pairs):
        cols = slice(p * LANES, (p + 1) * LANES)
        y = R["yacc"][:, cols]
        mean = _dot(y.astype(BF16), hsum) * inv_n
        d = y - mean
        var = _dot((d * d).astype(BF16), hsum) * inv_n
        yn = d * lax.rsqrt(var + RWKV_GN_EPS) * R["gnw"][:, cols] + R["gnb"][:, cols]
        z = src_ref[:, D_SHIFT + p * LANES:D_SHIFT + (p + 1) * LANES]
        R["y"][0, :, cols] = ((yn + R["bonus"][:, cols]) * _silu(z)).astype(R["y"].dtype)
        tick()
    tick(len(pending))


def _chunk_tri(n, chunk):
    i = jnp.arange(n)[:, None]
    j = jnp.arange(n)[None, :]
    return (((i // chunk) == (j // chunk)) & (j <= i)).astype(BF16)


def _rwkv(h_res, scale, shift, gain, w_rwkv, p, tt, out_dtype):
    bsz, t, d = h_res.shape
    n_t = t // tt
    n_tiles = bsz * n_t
    nxt = lambda s: jnp.minimum(s, n_tiles - 1)
    cur = lambda s: jnp.maximum(s - 1, 0)
    c2 = lambda s: (0, 0)
    vec = lambda n: pl.BlockSpec((1, n), c2)
    hsum = ((jnp.arange(LANES)[:, None] // RWKV_HEAD) ==
            (jnp.arange(LANES)[None, :] // RWKV_HEAD)).astype(BF16)
    lora = jnp.stack([
        jnp.concatenate([p["w_decay_up"], jnp.zeros((LORA, D_RWKV), F32)], axis=0),
        jnp.concatenate([jnp.zeros((LORA, D_RWKV), F32), p["w_icl_up"]], axis=0)]).astype(BF16)
    v1 = lambda a: a.reshape(1, -1).astype(F32)
    act = lambda: pltpu.VMEM((tt, D_RWKV), BF16)
    f32_tile = lambda: pltpu.VMEM((tt, D_RWKV), F32)
    return pl.pallas_call(
        functools.partial(_rwkv_kernel, tt=tt, n_t=n_t),
        grid=(n_tiles + 1,),
        in_specs=[pl.BlockSpec((1, tt, d), lambda s: (nxt(s) // n_t, nxt(s) % n_t, 0)),
                  pl.BlockSpec((1, 1, d), lambda s: (nxt(s) // n_t, 0, 0)),
                  pl.BlockSpec((1, 1, d), lambda s: (nxt(s) // n_t, 0, 0)),
                  vec(d),
                  pl.BlockSpec((d, RWKV_COLS), c2, pipeline_mode=pl.Buffered(1)),
                  vec(D_SHIFT),
                  pl.BlockSpec((2, 2 * LORA, D_RWKV), lambda s: (0, 0, 0)),
                  vec(D_RWKV), vec(D_RWKV), vec(D_RWKV), vec(D_RWKV), vec(D_RWKV),
                  vec(D_RWKV), vec(D_RWKV),
                  pl.BlockSpec((LANES, LANES), c2),
                  pl.BlockSpec((tt, tt), c2)],
        out_specs=pl.BlockSpec((1, tt, D_RWKV), lambda s: (cur(s) // n_t, cur(s) % n_t, 0)),
        out_shape=jax.ShapeDtypeStruct((bsz, t, D_RWKV), out_dtype),
        scratch_shapes=[pltpu.VMEM((tt, RWKV_COLS), F32),
                        pltpu.VMEM((tt, RWKV_COLS), F32),
                        pltpu.VMEM((1, D_SHIFT), F32),
                        pltpu.VMEM((D_RWKV // LANES, LANES, LANES), F32),
                        act(), act(), act(), act(), act(), act(), act(),
                        f32_tile(), f32_tile(), f32_tile()],
        compiler_params=pltpu.CompilerParams(dimension_semantics=("arbitrary",),
                                             vmem_limit_bytes=VMEM_LIMIT),
        name="rwkv7",
    )(h_res, scale.reshape(bsz, 1, d), shift.reshape(bsz, 1, d), gain.reshape(1, d), w_rwkv,
      v1(p["mu_rwkv"]), lora, v1(p["w_decay0"]), v1(p["a0"]), v1(p["k_k"]), v1(p["k_a"]),
      v1(p["r_k"]), v1(p["rwkv_gn_w"]), v1(p["rwkv_gn_b"]), hsum, _chunk_tri(tt, RWKV_CHUNK))


M_QK, M_V, M_O, M_Z, M_G = 0, 2 * D_MLSTM, 3 * D_MLSTM, 4 * D_MLSTM, 5 * D_MLSTM
MLSTM_COLS = 5 * D_MLSTM + GATE_PAD


def _mlstm_kernel(xn_ref, scale_ref, shift_ref, gain_ref, w_ref,
                  xo_ref, gate_ref, yr_ref, wout_ref, fg_ref,
                  convw_ref, convb_ref, gbias_ref, lnw_ref, skip_ref, tril_ref,
                  o_ref,
                  proj_a, proj_b, ym_a, ym_b, hres_ref, xbuf_ref, state_ref, m_ref, *, tt, n_t, final):
    s = pl.program_id(0)

    @pl.when(s == 0)
    def _():
        proj_b[...] = jnp.zeros_like(proj_b)
        ym_a[...] = jnp.zeros_like(ym_a)

    @pl.when(jnp.logical_or(s == 0, lax.rem(s + n_t - 1, n_t) == 0))
    def _():
        xbuf_ref[0:8, :] = jnp.zeros((8, 2 * D_MLSTM), F32)
        state_ref[...] = jnp.zeros_like(state_ref)
        m_ref[...] = jnp.zeros_like(m_ref)

    R = dict(xn=xn_ref, scale=scale_ref, shift=shift_ref, gain=gain_ref, w=w_ref, xo=xo_ref,
             gate=gate_ref, yr=yr_ref, wout=wout_ref, fg=fg_ref, convw=convw_ref, convb=convb_ref,
             gbias=gbias_ref, lnw=lnw_ref, skip=skip_ref, tril=tril_ref, o=o_ref, hres=hres_ref,
             xbuf=xbuf_ref, state=state_ref, m=m_ref)

    @pl.when(s % 2 == 0)
    def _():
        _mlstm_step(R, proj_b, proj_a, ym_b, ym_a, tt, final)

    @pl.when(s % 2 == 1)
    def _():
        _mlstm_step(R, proj_a, proj_b, ym_a, ym_b, tt, final)


def _mlstm_step(R, src_ref, dst_ref, ym_dst, ym_src, tt, final):
    L = MLSTM_CHUNK
    H = MLSTM_HEADS
    dh = MLSTM_HEAD
    d_model = R["xn"].shape[-1]
    wide = 2 * LANES

    y_cat = jnp.concatenate([R["yr"][0], ym_src[...]], axis=1)
    hn = _adaln_norm(R["xn"][0], R["scale"][0], R["shift"][0], R["gain"][...]).astype(BF16)

    def out_chunk(o):
        mix = _dot(y_cat, R["wout"][:, o:o + wide])
        R["hres"][:, o:o + wide] = R["xo"][0, :, o:o + wide] + R["gate"][0, :, o:o + wide] * mix

    def proj_chunk(o):
        w = min(wide, MLSTM_COLS - o)
        dst_ref[:, o:o + w] = _dot(hn, R["w"][:, o:o + w])

    pending = ([functools.partial(out_chunk, o) for o in range(0, d_model, wide)] +
               [functools.partial(proj_chunk, o) for o in range(0, MLSTM_COLS, wide)])

    def tick(n=1):
        for _ in range(n):
            if pending:
                pending.pop(0)()

    qk_parts = []
    for o in range(0, 2 * D_MLSTM, wide):
        cols = slice(o, o + wide)
        R["xbuf"][8:8 + tt, cols] = src_ref[:, M_QK + o:M_QK + o + wide]
        xb = R["xbuf"][:, cols]
        acc = R["convb"][:, cols] + R["convw"][CONV_K - 1:CONV_K, cols] * xb[8:8 + tt]
        for j in range(CONV_K - 1):
            sh = CONV_K - 1 - j
            acc = acc + R["convw"][j:j + 1, cols] * pltpu.roll(xb, sh, 0)[8:8 + tt]
        R["xbuf"][0:8, cols] = xb[tt:tt + 8]
        qk_parts.append(_silu(acc))
        tick()
    qk = jnp.concatenate(qk_parts, axis=1)
    q_c = qk[:, :D_MLSTM]
    k_c = qk[:, D_MLSTM:] * F32(dh ** -0.5)

    gt = src_ref[:, M_G:M_G + GATE_PAD] + R["gbias"][...]
    lane = lax.broadcasted_iota(jnp.int32, gt.shape, 1)
    is_f = (lane >= H) & (lane < 2 * H)
    log_f = jnp.minimum(gt, 0.0) - jnp.log(1.0 + jnp.exp(-jnp.abs(gt)))
    gates = jnp.where(is_f, log_f, gt)
    bcum = _dot_split_rhs(R["tril"][...], gates, 2)

    ti = lax.broadcasted_iota(jnp.int32, (L, L), 0)
    si = lax.broadcasted_iota(jnp.int32, (L, L), 1)
    causal = si <= ti
    ones_l = jnp.ones((L, dh), BF16)
    n_chunks = tt // L
    units = [(ci, h) for ci in range(n_chunks) for h in range(H)]
    rows_of = lambda ci: slice(ci * L, (ci + 1) * L)
    cols_of = lambda h: slice(h * dh, (h + 1) * dh)

    g_rows = [jnp.transpose(gates[rows_of(ci)]) for ci in range(n_chunks)]
    b_rows = [jnp.transpose(bcum[rows_of(ci)]) for ci in range(n_chunks)]
    q = [q_c[rows_of(ci), cols_of(h)].astype(BF16) for ci, h in units]
    kh = [k_c[rows_of(ci), cols_of(h)] for ci, h in units]
    v_aug = [jnp.concatenate([src_ref[rows_of(ci), M_V + h * dh:M_V + (h + 1) * dh].astype(BF16),
                              ones_l], axis=1) for ci, h in units]
    b_col = [bcum[rows_of(ci), H + h:H + h + 1] for ci, h in units]
    i_col = [gates[rows_of(ci), h:h + 1] for ci, h in units]
    row_v = [g_rows[ci][h:h + 1, :] - b_rows[ci][H + h:H + h + 1, :] for ci, h in units]
    tick()
    d_log = [jnp.where(causal, bc + rv, -jnp.inf) for bc, rv in zip(b_col, row_v)]
    a_t = [jnp.max(d, axis=-1, keepdims=True) for d in d_log]
    tick()
    scores = []
    for q_, k_, d, a in zip(q, kh, d_log, a_t):
        scores.append((_dot_nt(q_, k_.astype(BF16)) * jnp.exp(d - a)).astype(BF16))
        if len(scores) % 2 == 0:
            tick()
    nd_loc = [_dot(s_, va) for s_, va in zip(scores, v_aug)]
    b_last = [bc[L - 1:L, :] for bc in b_col]
    a_last = [a[L - 1:L, :] for a in a_t]
    kw = [(k_ * jnp.exp(bl - bc + ic - al)).astype(BF16)
          for k_, bl, bc, ic, al in zip(kh, b_last, b_col, i_col, a_last)]
    upd = [_dot_tn(kw_, va) for kw_, va in zip(kw, v_aug)]
    tick()

    st = [R["state"][h] for h in range(H)]
    m_prev = [R["m"][h:h + 1, 0:1] for h in range(H)]
    for ci in range(n_chunks):
        ids = [ci * H + h for h in range(H)]
        inter_log = [b_col[i] + m_prev[h] for h, i in enumerate(ids)]
        m_t = [jnp.maximum(il, a_t[i]) for il, i in zip(inter_log, ids)]
        qs = [_dot(q[i], st[h].astype(BF16)) for h, i in enumerate(ids)]
        nd = [jnp.exp(a_t[i] - mt) * nd_loc[i] + jnp.exp(il - mt) * qs_
              for i, mt, il, qs_ in zip(ids, m_t, inter_log, qs)]
        hh = [x[:, :dh] / jnp.maximum(jnp.abs(x[:, dh:]), jnp.exp(-mt)) for x, mt in zip(nd, m_t)]
        m_new = [mt[L - 1:L, :] for mt in m_t]
        st = [jnp.exp(b_last[i] + mp - mn) * s_ + jnp.exp(a_last[i] - mn) * upd[i]
              for i, mp, mn, s_ in zip(ids, m_prev, m_new, st)]
        m_prev = m_new
        mean = [jnp.mean(x, axis=-1, keepdims=True) for x in hh]
        dv = [x - mu for x, mu in zip(hh, mean)]
        var = [jnp.mean(x * x, axis=-1, keepdims=True) for x in dv]
        for h in range(H):
            rows, cols = rows_of(ci), cols_of(h)
            hn_ = dv[h] * lax.rsqrt(var[h] + MLSTM_LN_EPS) * R["lnw"][:, cols]
            o_gate = _sigmoid(src_ref[rows, M_O + h * dh:M_O + (h + 1) * dh])
            out = o_gate * hn_ + R["skip"][:, cols] * q_c[rows, cols]
            z = src_ref[rows, M_Z + h * dh:M_Z + (h + 1) * dh]
            ym_dst[rows, cols] = (out * _silu(z)).astype(ym_dst.dtype)
            tick()
    for h in range(H):
        R["state"][h] = st[h]
        R["m"][h:h + 1, :] = jnp.broadcast_to(m_prev[h], (1, LANES))
    tick(len(pending))

    hres = R["hres"][...]
    if final:
        ms = jnp.mean(hres * hres, axis=-1, keepdims=True)
        hres = hres * lax.rsqrt(ms + NORM_EPS) * R["fg"][...]
    R["o"][0] = hres.astype(R["o"].dtype)


def _mlstm_out(h_res, scale, shift, gain, w_m, y_r, gate, w_out, final_gain, p, tt, final, out_dtype):
    bsz, t, d = h_res.shape
    n_t = t // tt
    n_tiles = bsz * n_t
    nxt = lambda s: jnp.minimum(s, n_tiles - 1)
    old = lambda s: jnp.clip(s - 2, 0, n_tiles - 1)
    c2 = lambda s: (0, 0)
    vec = lambda n: pl.BlockSpec((1, n), c2)
    v1 = lambda a: a.reshape(1, -1).astype(F32)
    gbias = jnp.concatenate([p["mlstm_b_i"], p["mlstm_b_f"],
                             jnp.zeros((GATE_PAD - 2 * MLSTM_HEADS,), F32)]).reshape(1, GATE_PAD)
    tile_of = lambda f, n: pl.BlockSpec((1, tt, n), lambda s: (f(s) // n_t, f(s) % n_t, 0))
    batch_of = lambda f: pl.BlockSpec((1, 1, d), lambda s: (f(s) // n_t, 0, 0))
    resident = lambda shape: pl.BlockSpec(shape, c2, pipeline_mode=pl.Buffered(1))
    return pl.pallas_call(
        functools.partial(_mlstm_kernel, tt=tt, n_t=n_t, final=final),
        grid=(n_tiles + 2,),
        in_specs=[tile_of(nxt, d), batch_of(nxt), batch_of(nxt), vec(d), resident((d, MLSTM_COLS)),
                  tile_of(old, d), batch_of(old), tile_of(old, D_RWKV),
                  resident((D_RWKV + D_MLSTM, d)), vec(d),
                  pl.BlockSpec((CONV_K, 2 * D_MLSTM), c2),
                  vec(2 * D_MLSTM), vec(GATE_PAD), vec(D_MLSTM), vec(D_MLSTM),
                  pl.BlockSpec((tt, tt), c2)],
        out_specs=tile_of(old, d),
        out_shape=jax.ShapeDtypeStruct((bsz, t, d), out_dtype),
        scratch_shapes=[pltpu.VMEM((tt, MLSTM_COLS), F32),
                        pltpu.VMEM((tt, MLSTM_COLS), F32),
                        pltpu.VMEM((tt, D_MLSTM), BF16),
                        pltpu.VMEM((tt, D_MLSTM), BF16),
                        pltpu.VMEM((tt, d), F32),
                        pltpu.VMEM((tt + 8, 2 * D_MLSTM), F32),
                        pltpu.VMEM((MLSTM_HEADS, MLSTM_HEAD, 2 * MLSTM_HEAD), F32),
                        pltpu.VMEM((8, LANES), F32)],
        compiler_params=pltpu.CompilerParams(dimension_semantics=("arbitrary",),
                                             vmem_limit_bytes=VMEM_LIMIT),
        name="mlstm_out",
    )(h_res, scale.reshape(bsz, 1, d), shift.reshape(bsz, 1, d), gain.reshape(1, d), w_m,
      h_res, gate.reshape(bsz, 1, d), y_r, w_out, final_gain.reshape(1, d),
      p["mlstm_conv_w"].astype(F32), v1(p["mlstm_conv_b"]), gbias,
      v1(p["mlstm_ln_w"]), v1(p["mlstm_skip"]), _chunk_tri(tt, MLSTM_CHUNK))


def _permute_w_in(w):
    o = D_SHIFT
    z_r = w[:, o:o + D_RWKV]; o += D_RWKV
    q_m = w[:, o:o + D_MLSTM]; o += D_MLSTM
    k_m = w[:, o:o + D_MLSTM]; o += D_MLSTM
    v_m = w[:, o:o + D_MLSTM]; o += D_MLSTM
    o_m = w[:, o:o + D_MLSTM]; o += D_MLSTM
    i_m = w[:, o:o + MLSTM_HEADS]; o += MLSTM_HEADS
    f_m = w[:, o:o + MLSTM_HEADS]; o += MLSTM_HEADS
    z_m = w[:, o:o + D_MLSTM]
    pad = jnp.zeros((w.shape[0], GATE_PAD - 2 * MLSTM_HEADS), w.dtype)
    return jnp.concatenate([w[:, :D_SHIFT], z_r, q_m, k_m, v_m, o_m, z_m, i_m, f_m, pad],
                           axis=1).astype(BF16)


def _tile(t, want):
    tile = min(t, want)
    assert t % tile == 0
    return tile


def kernel(x, c, w_ada, b_ada, norm_gain, w_in, mu_rwkv, w_decay_up, w_decay0, w_icl_up, a0, k_k, k_a, r_k, rwkv_gn_w, rwkv_gn_b, mlstm_conv_w, mlstm_conv_b, mlstm_b_i, mlstm_b_f, mlstm_ln_w, mlstm_skip, w_out, final_gain):
    bsz, t, d = x.shape
    depth = w_ada.shape[0]
    tt_r = _tile(t, 256)
    tt_m = _tile(t, 256)
    assert t % RWKV_CHUNK == 0 and t % MLSTM_CHUNK == 0
    h_res = x.astype(F32)
    c32 = c.astype(F32)
    for l in range(depth):
        ada = _ada(c32, w_ada[l], b_ada[l])
        shift, scale, gate = ada[:, :d], ada[:, d:2 * d], ada[:, 2 * d:]
        w_perm = _permute_w_in(w_in[l])
        rp = dict(mu_rwkv=mu_rwkv[l], w_decay_up=w_decay_up[l], w_decay0=w_decay0[l],
                  w_icl_up=w_icl_up[l], a0=a0[l], k_k=k_k[l], k_a=k_a[l], r_k=r_k[l],
                  rwkv_gn_w=rwkv_gn_w[l], rwkv_gn_b=rwkv_gn_b[l])
        y_r = _rwkv(h_res, scale, shift, norm_gain[l], w_perm[:, :RWKV_COLS], rp, tt_r, BF16)
        mp = dict(mlstm_conv_w=mlstm_conv_w[l], mlstm_conv_b=mlstm_conv_b[l], mlstm_b_i=mlstm_b_i[l],
                  mlstm_b_f=mlstm_b_f[l], mlstm_ln_w=mlstm_ln_w[l], mlstm_skip=mlstm_skip[l])
        final = l == depth - 1
        h_res = _mlstm_out(h_res, scale, shift, norm_gain[l], w_perm[:, RWKV_COLS:], y_r, gate,
                           w_out[l].astype(BF16), final_gain, mp, tt_m, final,
                           x.dtype if final else F32)
    return h_res
```

```python
import functools

import jax
import jax.numpy as jnp
from jax import lax
from jax.experimental import pallas as pl
from jax.experimental.pallas import tpu as pltpu

F32 = jnp.float32
BF16 = jnp.bfloat16

RWKV_HEAD = 64
RWKV_HEADS = 8
D_RWKV = RWKV_HEAD * RWKV_HEADS
MLSTM_HEADS = 4
MLSTM_HEAD = 128
D_MLSTM = MLSTM_HEADS * MLSTM_HEAD
LORA = 64
D_SHIFT = 3 * D_RWKV + 2 * LORA
CONV_K = 4
NORM_EPS = 1e-6
RWKV_GN_EPS = 64e-5
MLSTM_LN_EPS = 1e-6
GATE_PAD = 128

RWKV_CHUNK = 64
MLSTM_CHUNK = 128
LANES = 128
VMEM_LIMIT = 56 * 1024 * 1024


def _dot(a, b):
    return jnp.dot(a, b, preferred_element_type=F32)


def _dot_nt(a, b):
    return lax.dot_general(a, b, (((1,), (1,)), ((), ())), preferred_element_type=F32)


def _dot_tn(a, b):
    return lax.dot_general(a, b, (((0,), (0,)), ((), ())), preferred_element_type=F32)


def _split_bf16(x, n):
    parts = []
    rem = x
    for i in range(n):
        p = rem.astype(BF16)
        parts.append(p)
        if i + 1 < n:
            rem = rem - p.astype(F32)
    return parts


def _dot_split_rhs(m, x, n):
    acc = None
    for p in _split_bf16(x, n):
        d = _dot(m, p)
        acc = d if acc is None else acc + d
    return acc


def _sigmoid(x):
    return 1.0 / (1.0 + jnp.exp(-x))


def _silu(x):
    return x * _sigmoid(x)


def _ada_kernel(c_ref, w_ref, b_ref, o_ref):
    ca = _silu(c_ref[...])
    c_hi, c_lo = _split_bf16(ca, 2)
    w = w_ref[...]
    w_hi, w_lo = _split_bf16(w, 2)
    o_ref[...] = _dot(c_hi, w_hi) + _dot(c_hi, w_lo) + _dot(c_lo, w_hi) + b_ref[...]


def _ada(c, w, b):
    bsz, d = c.shape
    n = w.shape[1]
    bn = d
    return pl.pallas_call(
        _ada_kernel,
        grid=(n // bn,),
        in_specs=[pl.BlockSpec((bsz, d), lambda j: (0, 0)),
                  pl.BlockSpec((d, bn), lambda j: (0, j)),
                  pl.BlockSpec((1, bn), lambda j: (0, j))],
        out_specs=pl.BlockSpec((bsz, bn), lambda j: (0, j)),
        out_shape=jax.ShapeDtypeStruct((bsz, n), F32),
        compiler_params=pltpu.CompilerParams(dimension_semantics=("arbitrary",),
                                             vmem_limit_bytes=VMEM_LIMIT),
        name="ada",
    )(c, w, b.reshape(1, n))


RWKV_COLS = D_SHIFT + D_RWKV


def _adaln_norm(x, scale, shift, gain):
    ms = jnp.mean(x * x, axis=-1, keepdims=True)
    return x * lax.rsqrt(ms + NORM_EPS) * gain * (1.0 + scale) + shift


def _pair_blockdiag(x):
    lane = lax.broadcasted_iota(jnp.int32, x.shape, 1)
    first = lane < RWKV_HEAD
    zero = jnp.zeros_like(x)
    return jnp.concatenate([jnp.where(first, x, zero), jnp.where(first, zero, x)], axis=0)


def _rwkv_kernel(x_ref, scale_ref, shift_ref, gain_ref, w_ref, mu_ref, lora_ref, w0_ref, a0_ref,
                 kk_ref, ka_ref, rk_ref, gnw_ref, gnb_ref, hsum_ref, tril_ref,
                 y_ref, hn_ref,
                 proj_a, proj_b, carry_ref, state_ref, at_ref, rt_ref, bt_ref, kt_ref, bp_ref, kp_ref,
                 vb_ref, gam_ref, yacc_ref, bonus_ref, *, tt, n_t):
    s = pl.program_id(0)

    @pl.when(s == 0)
    def _():
        proj_b[...] = jnp.zeros_like(proj_b)

    @pl.when(jnp.logical_or(s == 0, lax.rem(s + n_t - 1, n_t) == 0))
    def _():
        carry_ref[...] = jnp.zeros_like(carry_ref)
        state_ref[...] = jnp.zeros_like(state_ref)

    refs = dict(x=x_ref, scale=scale_ref, shift=shift_ref, gain=gain_ref, w=w_ref, mu=mu_ref,
                lora=lora_ref, w0=w0_ref, a0=a0_ref, kk=kk_ref, ka=ka_ref, rk=rk_ref, gnw=gnw_ref,
                gnb=gnb_ref, hsum=hsum_ref, tril=tril_ref, y=y_ref, hn=hn_ref, carry=carry_ref,
                state=state_ref,
                at=at_ref, rt=rt_ref, bt=bt_ref, kt=kt_ref, bp=bp_ref, kp=kp_ref, vb=vb_ref,
                gam=gam_ref, yacc=yacc_ref, bonus=bonus_ref)

    @pl.when(s % 2 == 0)
    def _():
        _rwkv_step(refs, proj_b, proj_a, tt)

    @pl.when(s % 2 == 1)
    def _():
        _rwkv_step(refs, proj_a, proj_b, tt)


def _rwkv_step(R, src_ref, dst_ref, tt):
    L = RWKV_CHUNK
    n_pairs = D_RWKV // LANES
    n_chunks = tt // L

    hn = _adaln_norm(R["x"][0], R["scale"][0], R["shift"][0], R["gain"][...]).astype(BF16)
    R["hn"][0] = hn
    pending = [(o, min(2 * LANES, RWKV_COLS - o)) for o in range(0, RWKV_COLS, 2 * LANES)]

    def tick(n=1):
        for _ in range(n):
            if pending:
                o, w = pending.pop(0)
                dst_ref[:, o:o + w] = _dot(hn, R["w"][:, o:o + w])

    row0 = lax.broadcasted_iota(jnp.int32, (tt, LANES), 0) == 0

    def shifted(g):
        cols = slice(g * LANES, (g + 1) * LANES)
        raw = src_ref[:, cols]
        prev = jnp.where(row0, R["carry"][:, cols], pltpu.roll(raw, 1, 0))
        R["carry"][:, cols] = raw[tt - 1:tt, :]
        return raw + R["mu"][:, cols] * (prev - raw)

    g_lora = shifted(3 * n_pairs)
    lw = _dot(jnp.tanh(g_lora).astype(BF16), R["lora"][0])
    la = _dot(g_lora.astype(BF16), R["lora"][1])
    hsum = R["hsum"][...]
    tick()
    for p in range(n_pairs):
        cols = slice(p * LANES, (p + 1) * LANES)
        r = shifted(p)
        k = shifted(n_pairs + p)
        v = shifted(2 * n_pairs + p)
        ld = -jnp.exp(F32(-0.5)) * _sigmoid(R["w0"][:, cols] + lw[:, cols])
        a = _sigmoid(R["a0"][:, cols] + la[:, cols])
        kk = k * R["kk"][:, cols]
        k2 = k * (1.0 + (a - 1.0) * R["ka"][:, cols])
        sums = _dot(jnp.concatenate([kk * kk, r * k2 * R["rk"][:, cols]], axis=1).astype(BF16), hsum)
        kk = kk / jnp.maximum(jnp.sqrt(sums[:, :LANES]), 1e-12)
        R["bonus"][:, cols] = sums[:, LANES:] * v
        kka = kk * a
        c2 = _dot(R["tril"][...], jnp.concatenate(_split_bf16(ld, 2), axis=1))
        c = c2[:, :LANES] + c2[:, LANES:]
        R["gam"][:, cols] = c
        c_last = jnp.concatenate(
            [jnp.broadcast_to(R["gam"][ci * L + L - 1:ci * L + L, cols], (L, LANES))
             for ci in range(n_chunks)], axis=0)
        e_pos = jnp.exp(c)
        e_neg = jnp.exp(-c)
        e_rev = jnp.exp(c_last - c)
        R["rt"][:, cols] = (r * e_pos).astype(BF16)
        R["kt"][:, cols] = (k2 * e_neg).astype(BF16)
        R["at"][:, cols] = (-kk * jnp.exp(c - ld)).astype(BF16)
        R["bt"][:, cols] = (kka * e_neg).astype(BF16)
        R["kp"][:, cols] = (k2 * e_rev).astype(BF16)
        R["bp"][:, cols] = (kka * e_rev).astype(BF16)
        R["vb"][:, cols] = v.astype(BF16)
        R["gam"][:, cols] = jnp.exp(c_last)
        tick(2)

    ti = lax.broadcasted_iota(jnp.int32, (L, LANES), 0)
    si = lax.broadcasted_iota(jnp.int32, (L, LANES), 1) % L
    strict = si < ti
    incl = si <= ti
    eye_pair = (si == ti).astype(F32)
    xr = ti ^ si
    r2 = lax.broadcasted_iota(jnp.int32, (LANES, LANES), 0)
    c2 = lax.broadcasted_iota(jnp.int32, (LANES, LANES), 1)
    same_head = (r2 < RWKV_HEAD) == (c2 < RWKV_HEAD)
    eye_full = r2 == c2

    chains = [(ci, p) for p in range(n_pairs) for ci in range(n_chunks)]

    def load(name):
        return [R[name][ci * L:(ci + 1) * L, p * LANES:(p + 1) * LANES] for ci, p in chains]

    at, rt, bt, kt, bp, kp, vb = (load(n) for n in ("at", "rt", "bt", "kt", "bp", "kp", "vb"))
    zero = jnp.zeros((L, LANES), F32)
    zero2 = jnp.zeros((LANES, LANES), F32)
    aa = [_dot_nt(jnp.concatenate([a_, r_], axis=0),
                  jnp.concatenate([_pair_blockdiag(b_), _pair_blockdiag(k_)], axis=0))
          for a_, r_, b_, k_ in zip(at, rt, bt, kt)]
    ab = [jnp.where(strict, x[:L, :LANES], zero) for x in aa]
    ak = [jnp.where(strict, x[:L, LANES:], zero).astype(BF16) for x in aa]
    rb = [jnp.where(incl, x[L:, :LANES], zero).astype(BF16) for x in aa]
    rk = [jnp.where(incl, x[L:, LANES:], zero).astype(BF16) for x in aa]
    tinv = [eye_pair + jnp.where((xr >> 1) == 0, x, zero) for x in ab]
    for lvl in range(1, 6):
        yy = [_dot(jnp.where((xr >> lvl) == 1, x, zero).astype(BF16), _pair_blockdiag(t.astype(BF16)))
              for x, t in zip(ab, tinv)]
        tinv = [t + _dot(t.astype(BF16), _pair_blockdiag(y.astype(BF16))) for t, y in zip(tinv, yy)]
    vbd = [_pair_blockdiag(x) for x in vb]
    w = [_dot(x, y).astype(BF16) for x, y in zip(ak, vbd)]
    ap = [_dot(t.astype(BF16), jnp.concatenate([_pair_blockdiag(a_), _pair_blockdiag(w_)], axis=1))
          for t, a_, w_ in zip(tinv, at, w)]
    ahat = [x[:, :LANES].astype(BF16) for x in ap]
    p1 = [x[:, LANES:].astype(BF16) for x in ap]
    q = [_dot(x, jnp.concatenate([_pair_blockdiag(a_), _pair_blockdiag(p_)], axis=1))
         for x, a_, p_ in zip(rb, ahat, p1)]
    y_in = [q_[:, LANES:] + _dot(x, y) for q_, x, y in zip(q, rk, vbd)]
    mg = [_dot_tn(jnp.concatenate([b_, k_], axis=0),
                  jnp.concatenate([jnp.concatenate([a_, p_], axis=1),
                                   jnp.concatenate([jnp.zeros_like(v_), v_], axis=1)], axis=0))
          for b_, k_, a_, p_, v_ in zip(bp, kp, ahat, p1, vb)]
    lhs = [jnp.concatenate([r_.astype(F32) + q_[:, :LANES], jnp.where(same_head, m_[:, :LANES], zero2)],
                           axis=0).astype(BF16) for r_, q_, m_ in zip(rt, q, mg)]
    gg = [jnp.where(same_head, m_[:, LANES:], zero2) for m_ in mg]
    gam_col = [jnp.sum(jnp.where(eye_full,
                                 jnp.broadcast_to(R["gam"][ci * L:ci * L + 1, p * LANES:(p + 1) * LANES],
                                                  (LANES, LANES)), zero2), axis=1, keepdims=True)
               for ci, p in chains]

    hs = [R["state"][p] for p in range(n_pairs)]
    for ci in range(n_chunks):
        idx = [p * n_chunks + ci for p in range(n_pairs)]
        yh = [_dot(lhs[i], h.astype(BF16)) for i, h in zip(idx, hs)]
        for p, i in enumerate(idx):
            R["yacc"][ci * L:(ci + 1) * L, p * LANES:(p + 1) * LANES] = yh[p][:L] + y_in[i]
        hs = [gam_col[i] * h + y_[L:] + gg[i] for i, h, y_ in zip(idx, hs, yh)]
    for p in range(n_pairs):
        R["state"][p] = hs[p]

    inv_n = F32(1.0 / RWKV_HEAD)
    for o in range(0, D_RWKV, 2 * LANES):
        cols = slice(o, o + 2 * LANES)
        y = R["yacc"][:, cols]
        mean = _dot(y.astype(BF16), hsum) * inv_n
        d = y - mean
        var = _dot((d * d).astype(BF16), hsum) * inv_n
        yn = d * lax.rsqrt(var + RWKV_GN_EPS) * R["gnw"][:, cols] + R["gnb"][:, cols]
        z = src_ref[:, D_SHIFT + o:D_SHIFT + o + 2 * LANES]
        R["y"][0, :, cols] = ((yn + R["bonus"][:, cols]) * _silu(z)).astype(R["y"].dtype)
        tick(2)
    tick(len(pending))


def _chunk_tri(n, chunk):
    i = jnp.arange(n)[:, None]
    j = jnp.arange(n)[None, :]
    return (((i // chunk) == (j // chunk)) & (j <= i)).astype(BF16)


def _rwkv(h_res, ada, gain, w_rwkv, p, tt, out_dtype):
    bsz, t, d = h_res.shape
    n_t = t // tt
    n_tiles = bsz * n_t
    nxt = lambda s: jnp.minimum(s, n_tiles - 1)
    cur = lambda s: jnp.maximum(s - 1, 0)
    c2 = lambda s: (0, 0)
    vec = lambda n: pl.BlockSpec((1, n), c2)
    hsum = ((jnp.arange(2 * LANES)[:, None] // RWKV_HEAD) ==
            (jnp.arange(2 * LANES)[None, :] // RWKV_HEAD)).astype(BF16)
    lora = jnp.stack([
        jnp.concatenate([p["w_decay_up"], jnp.zeros((LORA, D_RWKV), F32)], axis=0),
        jnp.concatenate([jnp.zeros((LORA, D_RWKV), F32), p["w_icl_up"]], axis=0)]).astype(BF16)
    v1 = lambda a: a.reshape(1, -1).astype(F32)
    act = lambda: pltpu.VMEM((tt, D_RWKV), BF16)
    f32_tile = lambda: pltpu.VMEM((tt, D_RWKV), F32)
    return pl.pallas_call(
        functools.partial(_rwkv_kernel, tt=tt, n_t=n_t),
        grid=(n_tiles + 1,),
        in_specs=[pl.BlockSpec((1, tt, d), lambda s: (nxt(s) // n_t, nxt(s) % n_t, 0)),
                  pl.BlockSpec((1, 1, d), lambda s: (nxt(s) // n_t, 0, 1)),
                  pl.BlockSpec((1, 1, d), lambda s: (nxt(s) // n_t, 0, 0)),
                  vec(d),
                  pl.BlockSpec((d, RWKV_COLS), c2, pipeline_mode=pl.Buffered(1)),
                  vec(D_SHIFT),
                  pl.BlockSpec((2, 2 * LORA, D_RWKV), lambda s: (0, 0, 0)),
                  vec(D_RWKV), vec(D_RWKV), vec(D_RWKV), vec(D_RWKV), vec(D_RWKV),
                  vec(D_RWKV), vec(D_RWKV),
                  pl.BlockSpec((2 * LANES, 2 * LANES), c2),
                  pl.BlockSpec((tt, tt), c2)],
        out_specs=[pl.BlockSpec((1, tt, D_RWKV), lambda s: (cur(s) // n_t, cur(s) % n_t, 0)),
                   pl.BlockSpec((1, tt, d), lambda s: (nxt(s) // n_t, nxt(s) % n_t, 0))],
        out_shape=[jax.ShapeDtypeStruct((bsz, t, D_RWKV), out_dtype),
                   jax.ShapeDtypeStruct((bsz, t, d), BF16)],
        scratch_shapes=[pltpu.VMEM((tt, RWKV_COLS), F32),
                        pltpu.VMEM((tt, RWKV_COLS), F32),
                        pltpu.VMEM((1, D_SHIFT), F32),
                        pltpu.VMEM((D_RWKV // LANES, LANES, LANES), F32),
                        act(), act(), act(), act(), act(), act(), act(),
                        f32_tile(), f32_tile(), f32_tile()],
        compiler_params=pltpu.CompilerParams(dimension_semantics=("arbitrary",),
                                             vmem_limit_bytes=VMEM_LIMIT),
        name="rwkv7",
    )(h_res, ada.reshape(bsz, 1, 3 * d), ada.reshape(bsz, 1, 3 * d), gain.reshape(1, d), w_rwkv,
      v1(p["mu_rwkv"]), lora, v1(p["w_decay0"]), v1(p["a0"]), v1(p["k_k"]), v1(p["k_a"]),
      v1(p["r_k"]), v1(p["rwkv_gn_w"]), v1(p["rwkv_gn_b"]), hsum, _chunk_tri(tt, RWKV_CHUNK))


M_QK, M_V, M_O, M_Z, M_G = 0, 2 * D_MLSTM, 3 * D_MLSTM, 4 * D_MLSTM, 5 * D_MLSTM
MLSTM_COLS = 5 * D_MLSTM + GATE_PAD


def _mlstm_kernel(hn_ref, w_ref,
                  xo_ref, gate_ref, yr_ref, wout_ref, fg_ref,
                  convw_ref, convb_ref, gbias_ref, lnw_ref, skip_ref, tril_ref,
                  o_ref,
                  proj_a, proj_b, ym_a, ym_b, hres_ref, xbuf_ref, state_ref, m_ref, *, tt, n_t, final):
    s = pl.program_id(0)

    @pl.when(s == 0)
    def _():
        proj_b[...] = jnp.zeros_like(proj_b)
        ym_a[...] = jnp.zeros_like(ym_a)

    @pl.when(jnp.logical_or(s == 0, lax.rem(s + n_t - 1, n_t) == 0))
    def _():
        xbuf_ref[0:8, :] = jnp.zeros((8, 2 * D_MLSTM), F32)
        state_ref[...] = jnp.zeros_like(state_ref)
        m_ref[...] = jnp.zeros_like(m_ref)

    R = dict(hn=hn_ref, w=w_ref, xo=xo_ref,
             gate=gate_ref, yr=yr_ref, wout=wout_ref, fg=fg_ref, convw=convw_ref, convb=convb_ref,
             gbias=gbias_ref, lnw=lnw_ref, skip=skip_ref, tril=tril_ref, o=o_ref, hres=hres_ref,
             xbuf=xbuf_ref, state=state_ref, m=m_ref)

    @pl.when(s % 2 == 0)
    def _():
        _mlstm_step(R, proj_b, proj_a, ym_b, ym_a, tt, final)

    @pl.when(s % 2 == 1)
    def _():
        _mlstm_step(R, proj_a, proj_b, ym_a, ym_b, tt, final)


def _mlstm_step(R, src_ref, dst_ref, ym_dst, ym_src, tt, final):
    L = MLSTM_CHUNK
    H = MLSTM_HEADS
    dh = MLSTM_HEAD
    d_model = R["xo"].shape[-1]
    wide = 2 * LANES

    y_cat = jnp.concatenate([R["yr"][0], ym_src[...]], axis=1)
    hn = R["hn"][0]

    def out_chunk(o):
        mix = _dot(y_cat, R["wout"][:, o:o + wide])
        R["hres"][:, o:o + wide] = R["xo"][0, :, o:o + wide] + R["gate"][0, :, o:o + wide] * mix

    def proj_chunk(o):
        w = min(wide, MLSTM_COLS - o)
        dst_ref[:, o:o + w] = _dot(hn, R["w"][:, o:o + w])

    pending = ([functools.partial(out_chunk, o) for o in range(0, d_model, wide)] +
               [functools.partial(proj_chunk, o) for o in range(0, MLSTM_COLS, wide)])

    def tick(n=1):
        for _ in range(n):
            if pending:
                pending.pop(0)()

    qk_parts = []
    for o in range(0, 2 * D_MLSTM, wide):
        cols = slice(o, o + wide)
        R["xbuf"][8:8 + tt, cols] = src_ref[:, M_QK + o:M_QK + o + wide]
        xb = R["xbuf"][:, cols]
        acc = R["convb"][:, cols] + R["convw"][CONV_K - 1:CONV_K, cols] * xb[8:8 + tt]
        for j in range(CONV_K - 1):
            sh = CONV_K - 1 - j
            acc = acc + R["convw"][j:j + 1, cols] * pltpu.roll(xb, sh, 0)[8:8 + tt]
        R["xbuf"][0:8, cols] = xb[tt:tt + 8]
        qk_parts.append(_silu(acc))
        tick()
    qk = jnp.concatenate(qk_parts, axis=1)
    q_c = qk[:, :D_MLSTM]
    k_c = qk[:, D_MLSTM:] * F32(dh ** -0.5)

    gt = src_ref[:, M_G:M_G + GATE_PAD] + R["gbias"][...]
    lane = lax.broadcasted_iota(jnp.int32, gt.shape, 1)
    is_f = (lane >= H) & (lane < 2 * H)
    log_f = jnp.minimum(gt, 0.0) - jnp.log(1.0 + jnp.exp(-jnp.abs(gt)))
    gates = jnp.where(is_f, log_f, gt)
    bcum = _dot_split_rhs(R["tril"][...], gates, 2)

    ti = lax.broadcasted_iota(jnp.int32, (L, L), 0)
    si = lax.broadcasted_iota(jnp.int32, (L, L), 1)
    causal = si <= ti
    ones_l = jnp.ones((L, dh), BF16)
    n_chunks = tt // L
    units = [(ci, h) for ci in range(n_chunks) for h in range(H)]
    rows_of = lambda ci: slice(ci * L, (ci + 1) * L)
    cols_of = lambda h: slice(h * dh, (h + 1) * dh)

    g_rows = [jnp.transpose(gates[rows_of(ci)]) for ci in range(n_chunks)]
    b_rows = [jnp.transpose(bcum[rows_of(ci)]) for ci in range(n_chunks)]
    q = [q_c[rows_of(ci), cols_of(h)].astype(BF16) for ci, h in units]
    kh = [k_c[rows_of(ci), cols_of(h)] for ci, h in units]
    v_aug = [jnp.concatenate([src_ref[rows_of(ci), M_V + h * dh:M_V + (h + 1) * dh].astype(BF16),
                              ones_l], axis=1) for ci, h in units]
    b_col = [bcum[rows_of(ci), H + h:H + h + 1] for ci, h in units]
    i_col = [gates[rows_of(ci), h:h + 1] for ci, h in units]
    row_v = [g_rows[ci][h:h + 1, :] - b_rows[ci][H + h:H + h + 1, :] for ci, h in units]
    tick()
    d_log = [jnp.where(causal, bc + rv, -jnp.inf) for bc, rv in zip(b_col, row_v)]
    a_t = [jnp.max(d, axis=-1, keepdims=True) for d in d_log]
    tick()
    scores = []
    for q_, k_, d, a in zip(q, kh, d_log, a_t):
        scores.append((_dot_nt(q_, k_.astype(BF16)) * jnp.exp(d - a)).astype(BF16))
        if len(scores) % 2 == 0:
            tick()
    nd_loc = [_dot(s_, va) for s_, va in zip(scores, v_aug)]
    b_last = [bc[L - 1:L, :] for bc in b_col]
    a_last = [a[L - 1:L, :] for a in a_t]
    kw = [(k_ * jnp.exp(bl - bc + ic - al)).astype(BF16)
          for k_, bl, bc, ic, al in zip(kh, b_last, b_col, i_col, a_last)]
    upd = [_dot_tn(kw_, va) for kw_, va in zip(kw, v_aug)]
    tick()

    st = [R["state"][h] for h in range(H)]
    m_prev = [R["m"][h:h + 1, 0:1] for h in range(H)]
    for ci in range(n_chunks):
        ids = [ci * H + h for h in range(H)]
        inter_log = [b_col[i] + m_prev[h] for h, i in enumerate(ids)]
        m_t = [jnp.maximum(il, a_t[i]) for il, i in zip(inter_log, ids)]
        qs = [_dot(q[i], st[h].astype(BF16)) for h, i in enumerate(ids)]
        nd = [jnp.exp(a_t[i] - mt) * nd_loc[i] + jnp.exp(il - mt) * qs_
              for i, mt, il, qs_ in zip(ids, m_t, inter_log, qs)]
        hh = [x[:, :dh] / jnp.maximum(jnp.abs(x[:, dh:]), jnp.exp(-mt)) for x, mt in zip(nd, m_t)]
        m_new = [mt[L - 1:L, :] for mt in m_t]
        st = [jnp.exp(b_last[i] + mp - mn) * s_ + jnp.exp(a_last[i] - mn) * upd[i]
              for i, mp, mn, s_ in zip(ids, m_prev, m_new, st)]
        m_prev = m_new
        mean = [jnp.mean(x, axis=-1, keepdims=True) for x in hh]
        dv = [x - mu for x, mu in zip(hh, mean)]
        var = [jnp.mean(x * x, axis=-1, keepdims=True) for x in dv]
        for h in range(H):
            rows, cols = rows_of(ci), cols_of(h)
            hn_ = dv[h] * lax.rsqrt(var[h] + MLSTM_LN_EPS) * R["lnw"][:, cols]
            o_gate = _sigmoid(src_ref[rows, M_O + h * dh:M_O + (h + 1) * dh])
            out = o_gate * hn_ + R["skip"][:, cols] * q_c[rows, cols]
            z = src_ref[rows, M_Z + h * dh:M_Z + (h + 1) * dh]
            ym_dst[rows, cols] = (out * _silu(z)).astype(ym_dst.dtype)
            tick()
    for h in range(H):
        R["state"][h] = st[h]
        R["m"][h:h + 1, :] = jnp.broadcast_to(m_prev[h], (1, LANES))
    tick(len(pending))

    hres = R["hres"][...]
    if final:
        ms = jnp.mean(hres * hres, axis=-1, keepdims=True)
        hres = hres * lax.rsqrt(ms + NORM_EPS) * R["fg"][...]
    R["o"][0] = hres.astype(R["o"].dtype)


def _mlstm_out(h_res, hn, w_m, y_r, ada, w_out, final_gain, p, tt, final, out_dtype):
    bsz, t, d = h_res.shape
    n_t = t // tt
    n_tiles = bsz * n_t
    nxt = lambda s: jnp.minimum(s, n_tiles - 1)
    old = lambda s: jnp.clip(s - 2, 0, n_tiles - 1)
    c2 = lambda s: (0, 0)
    vec = lambda n: pl.BlockSpec((1, n), c2)
    v1 = lambda a: a.reshape(1, -1).astype(F32)
    gbias = jnp.concatenate([p["mlstm_b_i"], p["mlstm_b_f"],
                             jnp.zeros((GATE_PAD - 2 * MLSTM_HEADS,), F32)]).reshape(1, GATE_PAD)
    tile_of = lambda f, n: pl.BlockSpec((1, tt, n), lambda s: (f(s) // n_t, f(s) % n_t, 0))
    gate_of = lambda f: pl.BlockSpec((1, 1, d), lambda s: (f(s) // n_t, 0, 2))
    resident = lambda shape: pl.BlockSpec(shape, c2, pipeline_mode=pl.Buffered(1))
    return pl.pallas_call(
        functools.partial(_mlstm_kernel, tt=tt, n_t=n_t, final=final),
        grid=(n_tiles + 2,),
        in_specs=[tile_of(nxt, d), resident((d, MLSTM_COLS)),
                  tile_of(old, d), gate_of(old), tile_of(old, D_RWKV),
                  resident((D_RWKV + D_MLSTM, d)), vec(d),
                  pl.BlockSpec((CONV_K, 2 * D_MLSTM), c2),
                  vec(2 * D_MLSTM), vec(GATE_PAD), vec(D_MLSTM), vec(D_MLSTM),
                  pl.BlockSpec((tt, tt), c2)],
        out_specs=tile_of(old, d),
        out_shape=jax.ShapeDtypeStruct((bsz, t, d), out_dtype),
        scratch_shapes=[pltpu.VMEM((tt, MLSTM_COLS), F32),
                        pltpu.VMEM((tt, MLSTM_COLS), F32),
                        pltpu.VMEM((tt, D_MLSTM), BF16),
                        pltpu.VMEM((tt, D_MLSTM), BF16),
                        pltpu.VMEM((tt, d), F32),
                        pltpu.VMEM((tt + 8, 2 * D_MLSTM), F32),
                        pltpu.VMEM((MLSTM_HEADS, MLSTM_HEAD, 2 * MLSTM_HEAD), F32),
                        pltpu.VMEM((8, LANES), F32)],
        compiler_params=pltpu.CompilerParams(dimension_semantics=("arbitrary",),
                                             vmem_limit_bytes=VMEM_LIMIT),
        name="mlstm_out",
    )(hn, w_m, h_res, ada.reshape(bsz, 1, 3 * d), y_r, w_out, final_gain.reshape(1, d),
      p["mlstm_conv_w"].astype(F32), v1(p["mlstm_conv_b"]), gbias,
      v1(p["mlstm_ln_w"]), v1(p["mlstm_skip"]), _chunk_tri(tt, MLSTM_CHUNK))


def _split_w_in(w):
    o = RWKV_COLS + 4 * D_MLSTM
    gates_end = o + 2 * MLSTM_HEADS
    pad = jnp.zeros((w.shape[0], GATE_PAD - 2 * MLSTM_HEADS), w.dtype)
    w_rwkv = w[:, :RWKV_COLS]
    w_mlstm = jnp.concatenate([w[:, RWKV_COLS:o], w[:, gates_end:], w[:, o:gates_end], pad], axis=1)
    return w_rwkv.astype(BF16), w_mlstm.astype(BF16)


def _tile(t, want):
    tile = min(t, want)
    assert t % tile == 0
    return tile


def kernel(x, c, w_ada, b_ada, norm_gain, w_in, mu_rwkv, w_decay_up, w_decay0, w_icl_up, a0, k_k, k_a, r_k, rwkv_gn_w, rwkv_gn_b, mlstm_conv_w, mlstm_conv_b, mlstm_b_i, mlstm_b_f, mlstm_ln_w, mlstm_skip, w_out, final_gain):
    bsz, t, d = x.shape
    depth = w_ada.shape[0]
    tt_r = _tile(t, 256)
    tt_m = _tile(t, 256)
    assert t % RWKV_CHUNK == 0 and t % MLSTM_CHUNK == 0
    h_res = x.astype(F32)
    c32 = c.astype(F32)
    for l in range(depth):
        ada = _ada(c32, w_ada[l], b_ada[l])
        w_rwkv, w_mlstm = _split_w_in(w_in[l])
        rp = dict(mu_rwkv=mu_rwkv[l], w_decay_up=w_decay_up[l], w_decay0=w_decay0[l],
                  w_icl_up=w_icl_up[l], a0=a0[l], k_k=k_k[l], k_a=k_a[l], r_k=r_k[l],
                  rwkv_gn_w=rwkv_gn_w[l], rwkv_gn_b=rwkv_gn_b[l])
        y_r, hn = _rwkv(h_res, ada, norm_gain[l], w_rwkv, rp, tt_r, BF16)
        mp = dict(mlstm_conv_w=mlstm_conv_w[l], mlstm_conv_b=mlstm_conv_b[l], mlstm_b_i=mlstm_b_i[l],
                  mlstm_b_f=mlstm_b_f[l], mlstm_ln_w=mlstm_ln_w[l], mlstm_skip=mlstm_skip[l])
        final = l == depth - 1
        h_res = _mlstm_out(h_res, hn, w_mlstm, y_r, ada,
                           w_out[l].astype(BF16), final_gain, mp, tt_m, final,
                           x.dtype if final else F32)
    return h_res
```

```python
import functools

import jax
import jax.numpy as jnp
from jax import lax
from jax.experimental import pallas as pl
from jax.experimental.pallas import tpu as pltpu

F32 = jnp.float32
BF16 = jnp.bfloat16

RWKV_HEAD = 64
RWKV_HEADS = 8
D_RWKV = RWKV_HEAD * RWKV_HEADS
MLSTM_HEADS = 4
MLSTM_HEAD = 128
D_MLSTM = MLSTM_HEADS * MLSTM_HEAD
LORA = 64
D_SHIFT = 3 * D_RWKV + 2 * LORA
CONV_K = 4
NORM_EPS = 1e-6
RWKV_GN_EPS = 64e-5
MLSTM_LN_EPS = 1e-6
GATE_PAD = 128

RWKV_CHUNK = 64
MLSTM_CHUNK = 128
LANES = 128
VMEM_LIMIT = 56 * 1024 * 1024


def _dot(a, b):
    return jnp.dot(a, b, preferred_element_type=F32)


def _dot_nt(a, b):
    return lax.dot_general(a, b, (((1,), (1,)), ((), ())), preferred_element_type=F32)


def _dot_tn(a, b):
    return lax.dot_general(a, b, (((0,), (0,)), ((), ())), preferred_element_type=F32)


def _split_bf16(x, n):
    parts = []
    rem = x
    for i in range(n):
        p = rem.astype(BF16)
        parts.append(p)
        if i + 1 < n:
            rem = rem - p.astype(F32)
    return parts


def _dot_split_rhs(m, x, n):
    acc = None
    for p in _split_bf16(x, n):
        d = _dot(m, p)
        acc = d if acc is None else acc + d
    return acc


def _sigmoid(x):
    return 1.0 / (1.0 + jnp.exp(-x))


def _silu(x):
    return x * _sigmoid(x)


def _ada_kernel(c_ref, w_ref, b_ref, o_ref):
    ca = _silu(c_ref[...])
    c_hi, c_lo = _split_bf16(ca, 2)
    w = w_ref[...]
    w_hi, w_lo = _split_bf16(w, 2)
    o_ref[...] = _dot(c_hi, w_hi) + _dot(c_hi, w_lo) + _dot(c_lo, w_hi) + b_ref[...]


def _ada(c, w, b):
    bsz, d = c.shape
    n = w.shape[1]
    bn = d
    return pl.pallas_call(
        _ada_kernel,
        grid=(n // bn,),
        in_specs=[pl.BlockSpec((bsz, d), lambda j: (0, 0)),
                  pl.BlockSpec((d, bn), lambda j: (0, j)),
                  pl.BlockSpec((1, bn), lambda j: (0, j))],
        out_specs=pl.BlockSpec((bsz, bn), lambda j: (0, j)),
        out_shape=jax.ShapeDtypeStruct((bsz, n), F32),
        compiler_params=pltpu.CompilerParams(dimension_semantics=("arbitrary",),
                                             vmem_limit_bytes=VMEM_LIMIT),
        name="ada",
    )(c, w, b.reshape(1, n))


RWKV_COLS = D_SHIFT + D_RWKV
MLSTM_COLS = 5 * D_MLSTM + GATE_PAD


def _adaln_norm(x, scale, shift, gain):
    ms = jnp.mean(x * x, axis=-1, keepdims=True)
    return x * lax.rsqrt(ms + NORM_EPS) * (gain * (1.0 + scale)) + shift


def _pair_blockdiag(x):
    lane = lax.broadcasted_iota(jnp.int32, x.shape, 1)
    first = lane < RWKV_HEAD
    zero = jnp.zeros_like(x)
    return jnp.concatenate([jnp.where(first, x, zero), jnp.where(first, zero, x)], axis=0)


def _rwkv_kernel(x_ref, scale_ref, shift_ref, gain_ref, w_ref, mu_ref, lora_ref, w0_ref, a0_ref,
                 kk_ref, ka_ref, rk_ref, gnw_ref, gnb_ref, hsum_ref, tril_ref,
                 y_ref, hn_ref,
                 proj_a, proj_b, carry_ref, state_ref, at_ref, rt_ref, bt_ref, kt_ref, bp_ref, kp_ref,
                 vb_ref, gam_ref, yacc_ref, bonus_ref, *, tt, n_t):
    s = pl.program_id(0)

    @pl.when(s == 0)
    def _():
        proj_b[...] = jnp.zeros_like(proj_b)

    @pl.when(jnp.logical_or(s == 0, lax.rem(s + n_t - 1, n_t) == 0))
    def _():
        carry_ref[...] = jnp.zeros_like(carry_ref)
        state_ref[...] = jnp.zeros_like(state_ref)

    refs = dict(x=x_ref, scale=scale_ref, shift=shift_ref, gain=gain_ref, w=w_ref, mu=mu_ref,
                lora=lora_ref, w0=w0_ref, a0=a0_ref, kk=kk_ref, ka=ka_ref, rk=rk_ref, gnw=gnw_ref,
                gnb=gnb_ref, hsum=hsum_ref, tril=tril_ref, y=y_ref, hn=hn_ref, carry=carry_ref,
                state=state_ref,
                at=at_ref, rt=rt_ref, bt=bt_ref, kt=kt_ref, bp=bp_ref, kp=kp_ref, vb=vb_ref,
                gam=gam_ref, yacc=yacc_ref, bonus=bonus_ref)

    @pl.when(s % 2 == 0)
    def _():
        _rwkv_step(refs, proj_b, proj_a, tt)

    @pl.when(s % 2 == 1)
    def _():
        _rwkv_step(refs, proj_a, proj_b, tt)


def _rwkv_step(R, src_ref, dst_ref, tt):
    L = RWKV_CHUNK
    n_pairs = D_RWKV // LANES
    n_chunks = tt // L

    hn = _adaln_norm(R["x"][0], R["scale"][0], R["shift"][0], R["gain"][...]).astype(BF16)
    R["hn"][0] = hn
    pending = [(o, min(2 * LANES, RWKV_COLS - o)) for o in range(0, RWKV_COLS, 2 * LANES)]

    def tick(n=1):
        for _ in range(n):
            if pending:
                o, w = pending.pop(0)
                dst_ref[:, o:o + w] = _dot(hn, R["w"][:, o:o + w])

    row0 = lax.broadcasted_iota(jnp.int32, (tt, LANES), 0) == 0

    def shifted(g):
        cols = slice(g * LANES, (g + 1) * LANES)
        raw = src_ref[:, cols]
        prev = jnp.where(row0, R["carry"][:, cols], pltpu.roll(raw, 1, 0))
        R["carry"][:, cols] = raw[tt - 1:tt, :]
        return raw + R["mu"][:, cols] * (prev - raw)

    g_lora = shifted(3 * n_pairs)
    lw = _dot(jnp.tanh(g_lora).astype(BF16), R["lora"][0])
    la = _dot(g_lora.astype(BF16), R["lora"][1])
    hsum = R["hsum"][...]
    tick()
    for p in range(n_pairs):
        cols = slice(p * LANES, (p + 1) * LANES)
        r = shifted(p)
        k = shifted(n_pairs + p)
        v = shifted(2 * n_pairs + p)
        ld = -jnp.exp(F32(-0.5)) * _sigmoid(R["w0"][:, cols] + lw[:, cols])
        a = _sigmoid(R["a0"][:, cols] + la[:, cols])
        kk = k * R["kk"][:, cols]
        k2 = k * (1.0 + (a - 1.0) * R["ka"][:, cols])
        sums = _dot(jnp.concatenate([kk * kk, r * k2 * R["rk"][:, cols]], axis=1).astype(BF16), hsum)
        kk = kk / jnp.maximum(jnp.sqrt(sums[:, :LANES]), 1e-12)
        R["bonus"][:, cols] = sums[:, LANES:] * v
        kka = kk * a
        c2 = _dot(R["tril"][...], jnp.concatenate(_split_bf16(ld, 2), axis=1))
        c = c2[:, :LANES] + c2[:, LANES:]
        R["gam"][:, cols] = c
        c_last = jnp.concatenate(
            [jnp.broadcast_to(R["gam"][ci * L + L - 1:ci * L + L, cols], (L, LANES))
             for ci in range(n_chunks)], axis=0)
        gam = jnp.exp(c_last)
        e_pos = jnp.exp(c)
        e_neg = 1.0 / e_pos
        e_rev = gam * e_neg
        R["rt"][:, cols] = (r * e_pos).astype(BF16)
        R["kt"][:, cols] = (k2 * e_neg).astype(BF16)
        R["at"][:, cols] = (-kk * jnp.exp(c - ld)).astype(BF16)
        R["bt"][:, cols] = (kka * e_neg).astype(BF16)
        R["kp"][:, cols] = (k2 * e_rev).astype(BF16)
        R["bp"][:, cols] = (kka * e_rev).astype(BF16)
        R["vb"][:, cols] = v.astype(BF16)
        R["gam"][:, cols] = gam
        tick(2)

    ti = lax.broadcasted_iota(jnp.int32, (L, LANES), 0)
    si = lax.broadcasted_iota(jnp.int32, (L, LANES), 1) % L
    strict = si < ti
    incl = si <= ti
    eye_pair = (si == ti).astype(F32)
    xr = ti ^ si
    r2 = lax.broadcasted_iota(jnp.int32, (LANES, LANES), 0)
    c2 = lax.broadcasted_iota(jnp.int32, (LANES, LANES), 1)
    same_head = (r2 < RWKV_HEAD) == (c2 < RWKV_HEAD)
    eye_full = r2 == c2

    chains = [(ci, p) for p in range(n_pairs) for ci in range(n_chunks)]

    def load(name):
        return [R[name][ci * L:(ci + 1) * L, p * LANES:(p + 1) * LANES] for ci, p in chains]

    at, rt, bt, kt, bp, kp, vb = (load(n) for n in ("at", "rt", "bt", "kt", "bp", "kp", "vb"))
    zero = jnp.zeros((L, LANES), F32)
    zero2 = jnp.zeros((LANES, LANES), F32)
    aa = [_dot_nt(jnp.concatenate([a_, r_], axis=0),
                  jnp.concatenate([_pair_blockdiag(b_), _pair_blockdiag(k_)], axis=0))
          for a_, r_, b_, k_ in zip(at, rt, bt, kt)]
    ab = [jnp.where(strict, x[:L, :LANES], zero) for x in aa]
    ak = [jnp.where(strict, x[:L, LANES:], zero).astype(BF16) for x in aa]
    rb = [jnp.where(incl, x[L:, :LANES], zero).astype(BF16) for x in aa]
    rk = [jnp.where(incl, x[L:, LANES:], zero).astype(BF16) for x in aa]
    tinv = [eye_pair + jnp.where((xr >> 1) == 0, x, zero) for x in ab]
    for lvl in range(1, 6):
        yy = [_dot(jnp.where((xr >> lvl) == 1, x, zero).astype(BF16), _pair_blockdiag(t.astype(BF16)))
              for x, t in zip(ab, tinv)]
        tinv = [t + _dot(t.astype(BF16), _pair_blockdiag(y.astype(BF16))) for t, y in zip(tinv, yy)]
    vbd = [_pair_blockdiag(x) for x in vb]
    w = [_dot(x, y).astype(BF16) for x, y in zip(ak, vbd)]
    ap = [_dot(t.astype(BF16), jnp.concatenate([_pair_blockdiag(a_), _pair_blockdiag(w_)], axis=1))
          for t, a_, w_ in zip(tinv, at, w)]
    ahat = [x[:, :LANES].astype(BF16) for x in ap]
    p1 = [x[:, LANES:].astype(BF16) for x in ap]
    q = [_dot(x, jnp.concatenate([_pair_blockdiag(a_), _pair_blockdiag(p_)], axis=1))
         for x, a_, p_ in zip(rb, ahat, p1)]
    y_in = [q_[:, LANES:] + _dot(x, y) for q_, x, y in zip(q, rk, vbd)]
    mg = [_dot_tn(jnp.concatenate([b_, k_], axis=0),
                  jnp.concatenate([jnp.concatenate([a_, p_], axis=1),
                                   jnp.concatenate([jnp.zeros_like(v_), v_], axis=1)], axis=0))
          for b_, k_, a_, p_, v_ in zip(bp, kp, ahat, p1, vb)]
    lhs = [jnp.concatenate([r_.astype(F32) + q_[:, :LANES], jnp.where(same_head, m_[:, :LANES], zero2)],
                           axis=0).astype(BF16) for r_, q_, m_ in zip(rt, q, mg)]
    gg = [jnp.where(same_head, m_[:, LANES:], zero2) for m_ in mg]
    gam_col = [jnp.sum(jnp.where(eye_full,
                                 jnp.broadcast_to(R["gam"][ci * L:ci * L + 1, p * LANES:(p + 1) * LANES],
                                                  (LANES, LANES)), zero2), axis=1, keepdims=True)
               for ci, p in chains]

    hs = [R["state"][p] for p in range(n_pairs)]
    for ci in range(n_chunks):
        idx = [p * n_chunks + ci for p in range(n_pairs)]
        yh = [_dot(lhs[i], h.astype(BF16)) for i, h in zip(idx, hs)]
        for p, i in enumerate(idx):
            R["yacc"][ci * L:(ci + 1) * L, p * LANES:(p + 1) * LANES] = yh[p][:L] + y_in[i]
        hs = [gam_col[i] * h + y_[L:] + gg[i] for i, h, y_ in zip(idx, hs, yh)]
    for p in range(n_pairs):
        R["state"][p] = hs[p]

    inv_n = F32(1.0 / RWKV_HEAD)
    for o in range(0, D_RWKV, 2 * LANES):
        cols = slice(o, o + 2 * LANES)
        y = R["yacc"][:, cols]
        mean = _dot(y.astype(BF16), hsum) * inv_n
        d = y - mean
        var = _dot((d * d).astype(BF16), hsum) * inv_n
        yn = d * lax.rsqrt(var + RWKV_GN_EPS) * R["gnw"][:, cols] + R["gnb"][:, cols]
        z = src_ref[:, D_SHIFT + o:D_SHIFT + o + 2 * LANES]
        R["y"][0, :, cols] = ((yn + R["bonus"][:, cols]) * _silu(z)).astype(R["y"].dtype)
        tick(2)
    tick(len(pending))


def _chunk_tri(n, chunk):
    i = jnp.arange(n)[:, None]
    j = jnp.arange(n)[None, :]
    return (((i // chunk) == (j // chunk)) & (j <= i)).astype(BF16)


def _rwkv(h_res, ada, gain, w_all, p, tt, out_dtype):
    bsz, t, d = h_res.shape
    n_t = t // tt
    n_tiles = bsz * n_t
    nxt = lambda s: jnp.minimum(s, n_tiles - 1)
    cur = lambda s: jnp.maximum(s - 1, 0)
    c2 = lambda s: (0, 0)
    vec = lambda n: pl.BlockSpec((1, n), c2)
    hsum = ((jnp.arange(2 * LANES)[:, None] // RWKV_HEAD) ==
            (jnp.arange(2 * LANES)[None, :] // RWKV_HEAD)).astype(BF16)
    lora = jnp.stack([
        jnp.concatenate([p["w_decay_up"], jnp.zeros((LORA, D_RWKV), F32)], axis=0),
        jnp.concatenate([jnp.zeros((LORA, D_RWKV), F32), p["w_icl_up"]], axis=0)]).astype(BF16)
    v1 = lambda a: a.reshape(1, -1).astype(F32)
    act = lambda: pltpu.VMEM((tt, D_RWKV), BF16)
    f32_tile = lambda: pltpu.VMEM((tt, D_RWKV), F32)
    return pl.pallas_call(
        functools.partial(_rwkv_kernel, tt=tt, n_t=n_t),
        grid=(n_tiles + 1,),
        in_specs=[pl.BlockSpec((1, tt, d), lambda s: (nxt(s) // n_t, nxt(s) % n_t, 0)),
                  pl.BlockSpec((1, 1, d), lambda s: (nxt(s) // n_t, 0, 1)),
                  pl.BlockSpec((1, 1, d), lambda s: (nxt(s) // n_t, 0, 0)),
                  vec(d),
                  pl.BlockSpec((d, MLSTM_COLS), c2, pipeline_mode=pl.Buffered(1)),
                  vec(D_SHIFT),
                  pl.BlockSpec((2, 2 * LORA, D_RWKV), lambda s: (0, 0, 0)),
                  vec(D_RWKV), vec(D_RWKV), vec(D_RWKV), vec(D_RWKV), vec(D_RWKV),
                  vec(D_RWKV), vec(D_RWKV),
                  pl.BlockSpec((2 * LANES, 2 * LANES), c2),
                  pl.BlockSpec((tt, tt), c2)],
        out_specs=[pl.BlockSpec((1, tt, D_RWKV), lambda s: (cur(s) // n_t, cur(s) % n_t, 0)),
                   pl.BlockSpec((1, tt, d), lambda s: (nxt(s) // n_t, nxt(s) % n_t, 0))],
        out_shape=[jax.ShapeDtypeStruct((bsz, t, D_RWKV), out_dtype),
                   jax.ShapeDtypeStruct((bsz, t, d), BF16)],
        scratch_shapes=[pltpu.VMEM((tt, RWKV_COLS), F32),
                        pltpu.VMEM((tt, RWKV_COLS), F32),
                        pltpu.VMEM((1, D_SHIFT), F32),
                        pltpu.VMEM((D_RWKV // LANES, LANES, LANES), F32),
                        act(), act(), act(), act(), act(), act(), act(),
                        f32_tile(), f32_tile(), f32_tile()],
        compiler_params=pltpu.CompilerParams(dimension_semantics=("arbitrary",),
                                             vmem_limit_bytes=VMEM_LIMIT),
        name="rwkv7",
    )(h_res, ada.reshape(bsz, 1, 3 * d), ada.reshape(bsz, 1, 3 * d), gain.reshape(1, d), w_all,
      v1(p["mu_rwkv"]), lora, v1(p["w_decay0"]), v1(p["a0"]), v1(p["k_k"]), v1(p["k_a"]),
      v1(p["r_k"]), v1(p["rwkv_gn_w"]), v1(p["rwkv_gn_b"]), hsum, _chunk_tri(tt, RWKV_CHUNK))


M_QK, M_V, M_O, M_Z, M_G = 0, 2 * D_MLSTM, 3 * D_MLSTM, 4 * D_MLSTM, 5 * D_MLSTM


def _mlstm_kernel(hn_ref, w_ref,
                  xo_ref, gate_ref, yr_ref, wout_ref, fg_ref,
                  convw_ref, convb_ref, gbias_ref, lnw_ref, skip_ref, tril_ref,
                  o_ref,
                  proj_a, proj_b, ym_a, ym_b, hres_ref, xbuf_ref, state_ref, m_ref, *, tt, n_t, final):
    s = pl.program_id(0)

    @pl.when(s == 0)
    def _():
        proj_b[...] = jnp.zeros_like(proj_b)
        ym_a[...] = jnp.zeros_like(ym_a)

    @pl.when(jnp.logical_or(s == 0, lax.rem(s + n_t - 1, n_t) == 0))
    def _():
        xbuf_ref[0:8, :] = jnp.zeros((8, 2 * D_MLSTM), F32)
        state_ref[...] = jnp.zeros_like(state_ref)
        m_ref[...] = jnp.zeros_like(m_ref)

    R = dict(hn=hn_ref, w=w_ref, xo=xo_ref,
             gate=gate_ref, yr=yr_ref, wout=wout_ref, fg=fg_ref, convw=convw_ref, convb=convb_ref,
             gbias=gbias_ref, lnw=lnw_ref, skip=skip_ref, tril=tril_ref, o=o_ref, hres=hres_ref,
             xbuf=xbuf_ref, state=state_ref, m=m_ref)

    @pl.when(s % 2 == 0)
    def _():
        _mlstm_step(R, proj_b, proj_a, ym_b, ym_a, tt, final)

    @pl.when(s % 2 == 1)
    def _():
        _mlstm_step(R, proj_a, proj_b, ym_a, ym_b, tt, final)


def _mlstm_step(R, src_ref, dst_ref, ym_dst, ym_src, tt, final):
    L = MLSTM_CHUNK
    H = MLSTM_HEADS
    dh = MLSTM_HEAD
    d_model = R["xo"].shape[-1]
    wide = 2 * LANES

    y_cat = jnp.concatenate([R["yr"][0], ym_src[...]], axis=1)
    hn = R["hn"][0]

    def out_chunk(o):
        mix = _dot(y_cat, R["wout"][:, o:o + wide])
        R["hres"][:, o:o + wide] = R["xo"][0, :, o:o + wide] + R["gate"][0, :, o:o + wide] * mix

    def proj_chunk(o):
        w = min(wide, MLSTM_COLS - o)
        dst_ref[:, o:o + w] = _dot(hn, R["w"][:, o:o + w])

    pending = ([functools.partial(out_chunk, o) for o in range(0, d_model, wide)] +
               [functools.partial(proj_chunk, o) for o in range(0, MLSTM_COLS, wide)])

    def tick(n=1):
        for _ in range(n):
            if pending:
                pending.pop(0)()

    qk_parts = []
    for o in range(0, 2 * D_MLSTM, wide):
        cols = slice(o, o + wide)
        R["xbuf"][8:8 + tt, cols] = src_ref[:, M_QK + o:M_QK + o + wide]
        xb = R["xbuf"][:, cols]
        acc = R["convb"][:, cols] + R["convw"][CONV_K - 1:CONV_K, cols] * xb[8:8 + tt]
        for j in range(CONV_K - 1):
            sh = CONV_K - 1 - j
            acc = acc + R["convw"][j:j + 1, cols] * pltpu.roll(xb, sh, 0)[8:8 + tt]
        R["xbuf"][0:8, cols] = xb[tt:tt + 8]
        qk_parts.append(_silu(acc))
        tick()
    qk = jnp.concatenate(qk_parts, axis=1)
    q_c = qk[:, :D_MLSTM]
    k_c = qk[:, D_MLSTM:] * F32(dh ** -0.5)

    gt = src_ref[:, M_G:M_G + GATE_PAD] + R["gbias"][...]
    lane = lax.broadcasted_iota(jnp.int32, gt.shape, 1)
    is_f = (lane >= H) & (lane < 2 * H)
    log_f = jnp.minimum(gt, 0.0) - jnp.log(1.0 + jnp.exp(-jnp.abs(gt)))
    gates = jnp.where(is_f, log_f, gt)
    bcum = _dot_split_rhs(R["tril"][...], gates, 2)

    ti = lax.broadcasted_iota(jnp.int32, (L, L), 0)
    si = lax.broadcasted_iota(jnp.int32, (L, L), 1)
    causal = si <= ti
    ones_l = jnp.ones((L, dh), BF16)
    n_chunks = tt // L
    units = [(ci, h) for ci in range(n_chunks) for h in range(H)]
    rows_of = lambda ci: slice(ci * L, (ci + 1) * L)
    cols_of = lambda h: slice(h * dh, (h + 1) * dh)

    g_rows = [jnp.transpose(gates[rows_of(ci)]) for ci in range(n_chunks)]
    b_rows = [jnp.transpose(bcum[rows_of(ci)]) for ci in range(n_chunks)]
    q = [q_c[rows_of(ci), cols_of(h)] for ci, h in units]
    kh = [k_c[rows_of(ci), cols_of(h)] for ci, h in units]
    v_aug = [jnp.concatenate([src_ref[rows_of(ci), M_V + h * dh:M_V + (h + 1) * dh].astype(BF16),
                              ones_l], axis=1) for ci, h in units]
    b_col = [bcum[rows_of(ci), H + h:H + h + 1] for ci, h in units]
    i_col = [gates[rows_of(ci), h:h + 1] for ci, h in units]
    row_v = [g_rows[ci][h:h + 1, :] - b_rows[ci][H + h:H + h + 1, :] for ci, h in units]
    tick()
    d_log = [jnp.where(causal, bc + rv, -jnp.inf) for bc, rv in zip(b_col, row_v)]
    a_t = [jnp.max(d, axis=-1, keepdims=True) for d in d_log]
    tick()
    scores = []
    for q_, k_, d, a in zip(q, kh, d_log, a_t):
        scores.append(_dot_nt(q_.astype(BF16), k_.astype(BF16)) * jnp.exp(d - a))
        if len(scores) % 2 == 0:
            tick()
    b_last = [bc[L - 1:L, :] for bc in b_col]
    a_last = [a[L - 1:L, :] for a in a_t]
    kw = [(k_ * jnp.exp(bl - bc + ic - al)).astype(BF16)
          for k_, bl, bc, ic, al in zip(kh, b_last, b_col, i_col, a_last)]
    upd = [_dot_tn(kw_, va) for kw_, va in zip(kw, v_aug)]
    tick()

    st = [R["state"][h] for h in range(H)]
    m_prev = [R["m"][h:h + 1, 0:1] for h in range(H)]
    for ci in range(n_chunks):
        ids = [ci * H + h for h in range(H)]
        inter_log = [b_col[i] + m_prev[h] for h, i in enumerate(ids)]
        m_t = [jnp.maximum(il, a_t[i]) for il, i in zip(inter_log, ids)]
        nd = [_dot(jnp.concatenate([(jnp.exp(a_t[i] - mt) * scores[i]).astype(BF16),
                                    (jnp.exp(il - mt) * q[i]).astype(BF16)], axis=1),
                   jnp.concatenate([v_aug[i], st[h].astype(BF16)], axis=0))
              for h, (i, mt, il) in enumerate(zip(ids, m_t, inter_log))]
        hh = [x[:, :dh] / jnp.maximum(jnp.abs(x[:, dh:]), jnp.exp(-mt)) for x, mt in zip(nd, m_t)]
        m_new = [mt[L - 1:L, :] for mt in m_t]
        st = [jnp.exp(b_last[i] + mp - mn) * s_ + jnp.exp(a_last[i] - mn) * upd[i]
              for i, mp, mn, s_ in zip(ids, m_prev, m_new, st)]
        m_prev = m_new
        mean = [jnp.mean(x, axis=-1, keepdims=True) for x in hh]
        dv = [x - mu for x, mu in zip(hh, mean)]
        var = [jnp.mean(x * x, axis=-1, keepdims=True) for x in dv]
        for h in range(H):
            rows, cols = rows_of(ci), cols_of(h)
            hn_ = dv[h] * lax.rsqrt(var[h] + MLSTM_LN_EPS) * R["lnw"][:, cols]
            o_gate = _sigmoid(src_ref[rows, M_O + h * dh:M_O + (h + 1) * dh])
            out = o_gate * hn_ + R["skip"][:, cols] * q_c[rows, cols]
            z = src_ref[rows, M_Z + h * dh:M_Z + (h + 1) * dh]
            ym_dst[rows, cols] = (out * _silu(z)).astype(ym_dst.dtype)
            tick()
    for h in range(H):
        R["state"][h] = st[h]
        R["m"][h:h + 1, :] = jnp.broadcast_to(m_prev[h], (1, LANES))
    tick(len(pending))

    hres = R["hres"][...]
    if final:
        ms = jnp.mean(hres * hres, axis=-1, keepdims=True)
        hres = hres * lax.rsqrt(ms + NORM_EPS) * R["fg"][...]
    R["o"][0] = hres.astype(R["o"].dtype)


def _mlstm_out(h_res, hn, w_all, y_r, ada, w_out, final_gain, p, tt, final, out_dtype):
    bsz, t, d = h_res.shape
    n_t = t // tt
    n_tiles = bsz * n_t
    nxt = lambda s: jnp.minimum(s, n_tiles - 1)
    old = lambda s: jnp.clip(s - 2, 0, n_tiles - 1)
    c2 = lambda s: (0, 0)
    vec = lambda n: pl.BlockSpec((1, n), c2)
    v1 = lambda a: a.reshape(1, -1).astype(F32)
    gbias = jnp.concatenate([p["mlstm_b_i"], p["mlstm_b_f"],
                             jnp.zeros((GATE_PAD - 2 * MLSTM_HEADS,), F32)]).reshape(1, GATE_PAD)
    tile_of = lambda f, n: pl.BlockSpec((1, tt, n), lambda s: (f(s) // n_t, f(s) % n_t, 0))
    gate_of = lambda f: pl.BlockSpec((1, 1, d), lambda s: (f(s) // n_t, 0, 2))
    resident = lambda shape: pl.BlockSpec(shape, c2, pipeline_mode=pl.Buffered(1))
    return pl.pallas_call(
        functools.partial(_mlstm_kernel, tt=tt, n_t=n_t, final=final),
        grid=(n_tiles + 2,),
        in_specs=[tile_of(nxt, d),
                  pl.BlockSpec((d, MLSTM_COLS), lambda s: (0, 1), pipeline_mode=pl.Buffered(1)),
                  tile_of(old, d), gate_of(old), tile_of(old, D_RWKV),
                  resident((D_RWKV + D_MLSTM, d)), vec(d),
                  pl.BlockSpec((CONV_K, 2 * D_MLSTM), c2),
                  vec(2 * D_MLSTM), vec(GATE_PAD), vec(D_MLSTM), vec(D_MLSTM),
                  pl.BlockSpec((tt, tt), c2)],
        out_specs=tile_of(old, d),
        out_shape=jax.ShapeDtypeStruct((bsz, t, d), out_dtype),
        scratch_shapes=[pltpu.VMEM((tt, MLSTM_COLS), F32),
                        pltpu.VMEM((tt, MLSTM_COLS), F32),
                        pltpu.VMEM((tt, D_MLSTM), BF16),
                        pltpu.VMEM((tt, D_MLSTM), BF16),
                        pltpu.VMEM((tt, d), F32),
                        pltpu.VMEM((tt + 8, 2 * D_MLSTM), F32),
                        pltpu.VMEM((MLSTM_HEADS, MLSTM_HEAD, 2 * MLSTM_HEAD), F32),
                        pltpu.VMEM((8, LANES), F32)],
        compiler_params=pltpu.CompilerParams(dimension_semantics=("arbitrary",),
                                             vmem_limit_bytes=VMEM_LIMIT),
        name="mlstm_out",
    )(hn, w_all, h_res, ada.reshape(bsz, 1, 3 * d), y_r, w_out, final_gain.reshape(1, d),
      p["mlstm_conv_w"].astype(F32), v1(p["mlstm_conv_b"]), gbias,
      v1(p["mlstm_ln_w"]), v1(p["mlstm_skip"]), _chunk_tri(tt, MLSTM_CHUNK))


def _regroup_w_in(w):
    o = RWKV_COLS + 4 * D_MLSTM
    gates_end = o + 2 * MLSTM_HEADS
    zeros = lambda n: jnp.zeros((w.shape[0], n), w.dtype)
    return jnp.concatenate([w[:, :RWKV_COLS], zeros(MLSTM_COLS - RWKV_COLS),
                            w[:, RWKV_COLS:o], w[:, gates_end:], w[:, o:gates_end],
                            zeros(GATE_PAD - 2 * MLSTM_HEADS)], axis=1).astype(BF16)


def _tile(t, want):
    tile = min(t, want)
    assert t % tile == 0
    return tile


def kernel(x, c, w_ada, b_ada, norm_gain, w_in, mu_rwkv, w_decay_up, w_decay0, w_icl_up, a0, k_k, k_a, r_k, rwkv_gn_w, rwkv_gn_b, mlstm_conv_w, mlstm_conv_b, mlstm_b_i, mlstm_b_f, mlstm_ln_w, mlstm_skip, w_out, final_gain):
    bsz, t, d = x.shape
    depth = w_ada.shape[0]
    tt_r = _tile(t, 256)
    tt_m = _tile(t, 256)
    assert t % RWKV_CHUNK == 0 and t % MLSTM_CHUNK == 0
    h_res = x.astype(F32)
    c32 = c.astype(F32)
    for l in range(depth):
        ada = _ada(c32, w_ada[l], b_ada[l])
        w_all = _regroup_w_in(w_in[l])
        rp = dict(mu_rwkv=mu_rwkv[l], w_decay_up=w_decay_up[l], w_decay0=w_decay0[l],
                  w_icl_up=w_icl_up[l], a0=a0[l], k_k=k_k[l], k_a=k_a[l], r_k=r_k[l],
                  rwkv_gn_w=rwkv_gn_w[l], rwkv_gn_b=rwkv_gn_b[l])
        y_r, hn = _rwkv(h_res, ada, norm_gain[l], w_all, rp, tt_r, BF16)
        mp = dict(mlstm_conv_w=mlstm_conv_w[l], mlstm_conv_b=mlstm_conv_b[l], mlstm_b_i=mlstm_b_i[l],
                  mlstm_b_f=mlstm_b_f[l], mlstm_ln_w=mlstm_ln_w[l], mlstm_skip=mlstm_skip[l])
        final = l == depth - 1
        h_res = _mlstm_out(h_res, hn, w_all, y_r, ada,
                           w_out[l].astype(BF16), final_gain, mp, tt_m, final,
                           x.dtype if final else F32)
    return h_res
```

```python
import functools

import jax
import jax.numpy as jnp
from jax import lax
from jax.experimental import pallas as pl
from jax.experimental.pallas import tpu as pltpu

F32 = jnp.float32
BF16 = jnp.bfloat16

RWKV_HEAD = 64
RWKV_HEADS = 8
D_RWKV = RWKV_HEAD * RWKV_HEADS
MLSTM_HEADS = 4
MLSTM_HEAD = 128
D_MLSTM = MLSTM_HEADS * MLSTM_HEAD
LORA = 64
D_SHIFT = 3 * D_RWKV + 2 * LORA
CONV_K = 4
NORM_EPS = 1e-6
RWKV_GN_EPS = 64e-5
MLSTM_LN_EPS = 1e-6
GATE_PAD = 128

RWKV_CHUNK = 64
MLSTM_CHUNK = 128
LANES = 128
VMEM_LIMIT = 56 * 1024 * 1024


def _dot(a, b):
    return jnp.dot(a, b, preferred_element_type=F32)


def _dot_nt(a, b):
    return lax.dot_general(a, b, (((1,), (1,)), ((), ())), preferred_element_type=F32)


def _dot_tn(a, b):
    return lax.dot_general(a, b, (((0,), (0,)), ((), ())), preferred_element_type=F32)


def _split_bf16(x, n):
    parts = []
    rem = x
    for i in range(n):
        p = rem.astype(BF16)
        parts.append(p)
        if i + 1 < n:
            rem = rem - p.astype(F32)
    return parts


def _dot_split_rhs(m, x, n):
    acc = None
    for p in _split_bf16(x, n):
        d = _dot(m, p)
        acc = d if acc is None else acc + d
    return acc


def _sigmoid(x):
    return 1.0 / (1.0 + jnp.exp(-x))


def _silu(x):
    return x * _sigmoid(x)


def _ada_kernel(c_ref, w_ref, b_ref, o_ref):
    ca = _silu(c_ref[...])
    c_hi, c_lo = _split_bf16(ca, 2)
    w = w_ref[...]
    w_hi, w_lo = _split_bf16(w, 2)
    o_ref[...] = _dot(c_hi, w_hi) + _dot(c_hi, w_lo) + _dot(c_lo, w_hi) + b_ref[...]


def _ada(c, w, b):
    bsz, d = c.shape
    n = w.shape[1]
    bn = d
    return pl.pallas_call(
        _ada_kernel,
        grid=(n // bn,),
        in_specs=[pl.BlockSpec((bsz, d), lambda j: (0, 0)),
                  pl.BlockSpec((d, bn), lambda j: (0, j)),
                  pl.BlockSpec((1, bn), lambda j: (0, j))],
        out_specs=pl.BlockSpec((bsz, bn), lambda j: (0, j)),
        out_shape=jax.ShapeDtypeStruct((bsz, n), F32),
        compiler_params=pltpu.CompilerParams(dimension_semantics=("arbitrary",),
                                             vmem_limit_bytes=VMEM_LIMIT),
        name="ada",
    )(c, w, b.reshape(1, n))


RWKV_COLS = D_SHIFT + D_RWKV
MLSTM_COLS = 5 * D_MLSTM + GATE_PAD


def _adaln_norm(x, scale, shift, gain):
    ms = jnp.mean(x * x, axis=-1, keepdims=True)
    return x * lax.rsqrt(ms + NORM_EPS) * (gain * (1.0 + scale)) + shift


def _pair_blockdiag(x):
    lane = lax.broadcasted_iota(jnp.int32, x.shape, 1)
    first = lane < RWKV_HEAD
    zero = jnp.zeros_like(x)
    return jnp.concatenate([jnp.where(first, x, zero), jnp.where(first, zero, x)], axis=0)


def _rwkv_kernel(x_ref, scale_ref, shift_ref, gain_ref, w_ref, mu_ref, lora_ref, w0_ref, a0_ref,
                 kk_ref, ka_ref, rk_ref, gnw_ref, gnb_ref, hsum_ref, tril_ref,
                 y_ref, hn_ref,
                 proj_a, proj_b, carry_ref, state_ref, at_ref, rt_ref, bt_ref, kt_ref, bp_ref, kp_ref,
                 vb_ref, gam_ref, yacc_ref, bonus_ref, *, tt, n_t):
    s = pl.program_id(0)

    @pl.when(s == 0)
    def _():
        proj_b[...] = jnp.zeros_like(proj_b)

    @pl.when(jnp.logical_or(s == 0, lax.rem(s + n_t - 1, n_t) == 0))
    def _():
        carry_ref[...] = jnp.zeros_like(carry_ref)
        state_ref[...] = jnp.zeros_like(state_ref)

    refs = dict(x=x_ref, scale=scale_ref, shift=shift_ref, gain=gain_ref, w=w_ref, mu=mu_ref,
                lora=lora_ref, w0=w0_ref, a0=a0_ref, kk=kk_ref, ka=ka_ref, rk=rk_ref, gnw=gnw_ref,
                gnb=gnb_ref, hsum=hsum_ref, tril=tril_ref, y=y_ref, hn=hn_ref, carry=carry_ref,
                state=state_ref,
                at=at_ref, rt=rt_ref, bt=bt_ref, kt=kt_ref, bp=bp_ref, kp=kp_ref, vb=vb_ref,
                gam=gam_ref, yacc=yacc_ref, bonus=bonus_ref)

    @pl.when(s % 2 == 0)
    def _():
        _rwkv_step(refs, proj_b, proj_a, tt)

    @pl.when(s % 2 == 1)
    def _():
        _rwkv_step(refs, proj_a, proj_b, tt)


def _rwkv_step(R, src_ref, dst_ref, tt):
    L = RWKV_CHUNK
    n_pairs = D_RWKV // LANES
    n_chunks = tt // L

    hn = _adaln_norm(R["x"][0], R["scale"][0], R["shift"][0], R["gain"][...]).astype(BF16)
    R["hn"][0] = hn
    pending = [(o, min(2 * LANES, RWKV_COLS - o)) for o in range(0, RWKV_COLS, 2 * LANES)]

    def tick(n=1):
        for _ in range(n):
            if pending:
                o, w = pending.pop(0)
                dst_ref[:, o:o + w] = _dot(hn, R["w"][:, o:o + w])

    row0 = lax.broadcasted_iota(jnp.int32, (tt, LANES), 0) == 0

    def shifted(g):
        cols = slice(g * LANES, (g + 1) * LANES)
        raw = src_ref[:, cols]
        prev = jnp.where(row0, R["carry"][:, cols], pltpu.roll(raw, 1, 0))
        R["carry"][:, cols] = raw[tt - 1:tt, :]
        return raw + R["mu"][:, cols] * (prev - raw)

    g_lora = shifted(3 * n_pairs)
    lw = _dot(jnp.tanh(g_lora).astype(BF16), R["lora"][0])
    la = _dot(g_lora.astype(BF16), R["lora"][1])
    hsum = R["hsum"][...]
    tick()
    for p in range(n_pairs):
        cols = slice(p * LANES, (p + 1) * LANES)
        r = shifted(p)
        k = shifted(n_pairs + p)
        v = shifted(2 * n_pairs + p)
        ld = -jnp.exp(F32(-0.5)) * _sigmoid(R["w0"][:, cols] + lw[:, cols])
        a = _sigmoid(R["a0"][:, cols] + la[:, cols])
        kk = k * R["kk"][:, cols]
        k2 = k * (1.0 + (a - 1.0) * R["ka"][:, cols])
        sums = _dot(jnp.concatenate([kk * kk, r * k2 * R["rk"][:, cols]], axis=1).astype(BF16), hsum)
        kk = kk / jnp.maximum(jnp.sqrt(sums[:, :LANES]), 1e-12)
        R["bonus"][:, cols] = sums[:, LANES:] * v
        kka = kk * a
        c2 = _dot(R["tril"][...], jnp.concatenate(_split_bf16(ld, 2), axis=1))
        c = c2[:, :LANES] + c2[:, LANES:]
        R["gam"][:, cols] = c
        c_last = jnp.concatenate(
            [jnp.broadcast_to(R["gam"][ci * L + L - 1:ci * L + L, cols], (L, LANES))
             for ci in range(n_chunks)], axis=0)
        gam = jnp.exp(c_last)
        e_pos = jnp.exp(c)
        e_neg = 1.0 / e_pos
        e_rev = gam * e_neg
        R["rt"][:, cols] = (r * e_pos).astype(BF16)
        R["kt"][:, cols] = (k2 * e_neg).astype(BF16)
        R["at"][:, cols] = (-kk * jnp.exp(c - ld)).astype(BF16)
        R["bt"][:, cols] = (kka * e_neg).astype(BF16)
        R["kp"][:, cols] = (k2 * e_rev).astype(BF16)
        R["bp"][:, cols] = (kka * e_rev).astype(BF16)
        R["vb"][:, cols] = v.astype(BF16)
        R["gam"][:, cols] = gam
        tick(2)

    ti = lax.broadcasted_iota(jnp.int32, (L, LANES), 0)
    si = lax.broadcasted_iota(jnp.int32, (L, LANES), 1) % L
    strict = si < ti
    incl = si <= ti
    eye_pair = (si == ti).astype(F32)
    xr = ti ^ si
    r2 = lax.broadcasted_iota(jnp.int32, (LANES, LANES), 0)
    c2 = lax.broadcasted_iota(jnp.int32, (LANES, LANES), 1)
    same_head = (r2 < RWKV_HEAD) == (c2 < RWKV_HEAD)
    eye_full = r2 == c2

    chains = [(ci, p) for p in range(n_pairs) for ci in range(n_chunks)]

    def load(name):
        return [R[name][ci * L:(ci + 1) * L, p * LANES:(p + 1) * LANES] for ci, p in chains]

    at, rt, bt, kt, bp, kp, vb = (load(n) for n in ("at", "rt", "bt", "kt", "bp", "kp", "vb"))
    zero = jnp.zeros((L, LANES), F32)
    zero2 = jnp.zeros((LANES, LANES), F32)
    aa = [_dot_nt(jnp.concatenate([a_, r_], axis=0),
                  jnp.concatenate([_pair_blockdiag(b_), _pair_blockdiag(k_)], axis=0))
          for a_, r_, b_, k_ in zip(at, rt, bt, kt)]
    ab = [jnp.where(strict, x[:L, :LANES], zero) for x in aa]
    ak = [jnp.where(strict, x[:L, LANES:], zero).astype(BF16) for x in aa]
    rb = [jnp.where(incl, x[L:, :LANES], zero).astype(BF16) for x in aa]
    rk = [jnp.where(incl, x[L:, LANES:], zero).astype(BF16) for x in aa]
    tinv = [eye_pair + jnp.where((xr >> 1) == 0, x, zero) for x in ab]
    for lvl in range(1, 6):
        yy = [_dot(jnp.where((xr >> lvl) == 1, x, zero).astype(BF16), _pair_blockdiag(t.astype(BF16)))
              for x, t in zip(ab, tinv)]
        tinv = [t + _dot(t.astype(BF16), _pair_blockdiag(y.astype(BF16))) for t, y in zip(tinv, yy)]
    wv = [_dot(jnp.concatenate([x, z_], axis=0), _pair_blockdiag(y)) for x, z_, y in zip(ak, rk, vb)]
    w = [x[:L].astype(BF16) for x in wv]
    ap = [_dot(t.astype(BF16), jnp.concatenate([_pair_blockdiag(a_), _pair_blockdiag(w_)], axis=1))
          for t, a_, w_ in zip(tinv, at, w)]
    ahat = [x[:, :LANES].astype(BF16) for x in ap]
    p1 = [x[:, LANES:].astype(BF16) for x in ap]
    q = [_dot(x, jnp.concatenate([_pair_blockdiag(a_), _pair_blockdiag(p_)], axis=1))
         for x, a_, p_ in zip(rb, ahat, p1)]
    y_in = [q_[:, LANES:] + x[L:] for q_, x in zip(q, wv)]
    mg = [_dot_tn(jnp.concatenate([b_, k_], axis=0),
                  jnp.concatenate([jnp.concatenate([a_, p_], axis=1),
                                   jnp.concatenate([jnp.zeros_like(v_), v_], axis=1)], axis=0))
          for b_, k_, a_, p_, v_ in zip(bp, kp, ahat, p1, vb)]
    lhs = [jnp.concatenate([r_.astype(F32) + q_[:, :LANES], jnp.where(same_head, m_[:, :LANES], zero2)],
                           axis=0).astype(BF16) for r_, q_, m_ in zip(rt, q, mg)]
    gg = [jnp.where(same_head, m_[:, LANES:], zero2) for m_ in mg]
    gam_col = [jnp.sum(jnp.where(eye_full,
                                 jnp.broadcast_to(R["gam"][ci * L:ci * L + 1, p * LANES:(p + 1) * LANES],
                                                  (LANES, LANES)), zero2), axis=1, keepdims=True)
               for ci, p in chains]

    hs = [R["state"][p] for p in range(n_pairs)]
    for ci in range(n_chunks):
        idx = [p * n_chunks + ci for p in range(n_pairs)]
        yh = [_dot(lhs[i], h.astype(BF16)) for i, h in zip(idx, hs)]
        for p, i in enumerate(idx):
            R["yacc"][ci * L:(ci + 1) * L, p * LANES:(p + 1) * LANES] = yh[p][:L] + y_in[i]
        hs = [gam_col[i] * h + y_[L:] + gg[i] for i, h, y_ in zip(idx, hs, yh)]
    for p in range(n_pairs):
        R["state"][p] = hs[p]

    inv_n = F32(1.0 / RWKV_HEAD)
    for o in range(0, D_RWKV, 2 * LANES):
        cols = slice(o, o + 2 * LANES)
        y = R["yacc"][:, cols]
        mean = _dot(y.astype(BF16), hsum) * inv_n
        d = y - mean
        var = _dot((d * d).astype(BF16), hsum) * inv_n
        yn = d * lax.rsqrt(var + RWKV_GN_EPS) * R["gnw"][:, cols] + R["gnb"][:, cols]
        z = src_ref[:, D_SHIFT + o:D_SHIFT + o + 2 * LANES]
        R["y"][0, :, cols] = ((yn + R["bonus"][:, cols]) * _silu(z)).astype(R["y"].dtype)
        tick(2)
    tick(len(pending))


def _chunk_tri(n, chunk):
    i = jnp.arange(n)[:, None]
    j = jnp.arange(n)[None, :]
    return (((i // chunk) == (j // chunk)) & (j <= i)).astype(BF16)


def _rwkv(h_res, ada, gain, w_all, p, tt, out_dtype):
    bsz, t, d = h_res.shape
    n_t = t // tt
    n_tiles = bsz * n_t
    nxt = lambda s: jnp.minimum(s, n_tiles - 1)
    cur = lambda s: jnp.maximum(s - 1, 0)
    c2 = lambda s: (0, 0)
    vec = lambda n: pl.BlockSpec((1, n), c2)
    hsum = ((jnp.arange(2 * LANES)[:, None] // RWKV_HEAD) ==
            (jnp.arange(2 * LANES)[None, :] // RWKV_HEAD)).astype(BF16)
    lora = jnp.stack([
        jnp.concatenate([p["w_decay_up"], jnp.zeros((LORA, D_RWKV), F32)], axis=0),
        jnp.concatenate([jnp.zeros((LORA, D_RWKV), F32), p["w_icl_up"]], axis=0)]).astype(BF16)
    v1 = lambda a: a.reshape(1, -1).astype(F32)
    act = lambda: pltpu.VMEM((tt, D_RWKV), BF16)
    f32_tile = lambda: pltpu.VMEM((tt, D_RWKV), F32)
    return pl.pallas_call(
        functools.partial(_rwkv_kernel, tt=tt, n_t=n_t),
        grid=(n_tiles + 1,),
        in_specs=[pl.BlockSpec((1, tt, d), lambda s: (nxt(s) // n_t, nxt(s) % n_t, 0)),
                  pl.BlockSpec((1, 1, d), lambda s: (nxt(s) // n_t, 0, 1)),
                  pl.BlockSpec((1, 1, d), lambda s: (nxt(s) // n_t, 0, 0)),
                  vec(d),
                  pl.BlockSpec((d, MLSTM_COLS), c2, pipeline_mode=pl.Buffered(1)),
                  vec(D_SHIFT),
                  pl.BlockSpec((2, 2 * LORA, D_RWKV), lambda s: (0, 0, 0)),
                  vec(D_RWKV), vec(D_RWKV), vec(D_RWKV), vec(D_RWKV), vec(D_RWKV),
                  vec(D_RWKV), vec(D_RWKV),
                  pl.BlockSpec((2 * LANES, 2 * LANES), c2),
                  pl.BlockSpec((tt, tt), c2)],
        out_specs=[pl.BlockSpec((1, tt, D_RWKV), lambda s: (cur(s) // n_t, cur(s) % n_t, 0)),
                   pl.BlockSpec((1, tt, d), lambda s: (nxt(s) // n_t, nxt(s) % n_t, 0))],
        out_shape=[jax.ShapeDtypeStruct((bsz, t, D_RWKV), out_dtype),
                   jax.ShapeDtypeStruct((bsz, t, d), BF16)],
        scratch_shapes=[pltpu.VMEM((tt, RWKV_COLS), F32),
                        pltpu.VMEM((tt, RWKV_COLS), F32),
                        pltpu.VMEM((1, D_SHIFT), F32),
                        pltpu.VMEM((D_RWKV // LANES, LANES, LANES), F32),
                        act(), act(), act(), act(), act(), act(), act(),
                        f32_tile(), f32_tile(), f32_tile()],
        compiler_params=pltpu.CompilerParams(dimension_semantics=("arbitrary",),
                                             vmem_limit_bytes=VMEM_LIMIT),
        name="rwkv7",
    )(h_res, ada.reshape(bsz, 1, 3 * d), ada.reshape(bsz, 1, 3 * d), gain.reshape(1, d), w_all,
      v1(p["mu_rwkv"]), lora, v1(p["w_decay0"]), v1(p["a0"]), v1(p["k_k"]), v1(p["k_a"]),
      v1(p["r_k"]), v1(p["rwkv_gn_w"]), v1(p["rwkv_gn_b"]), hsum, _chunk_tri(tt, RWKV_CHUNK))


M_QK, M_V, M_O, M_Z, M_G = 0, 2 * D_MLSTM, 3 * D_MLSTM, 4 * D_MLSTM, 5 * D_MLSTM


def _mlstm_kernel(hn_ref, w_ref,
                  xo_ref, gate_ref, yr_ref, wout_ref, fg_ref,
                  convw_ref, convb_ref, gbias_ref, lnw_ref, skip_ref, tril_ref,
                  o_ref,
                  proj_a, proj_b, ym_a, ym_b, hres_ref, xbuf_ref, state_ref, m_ref, *, tt, n_t, final):
    s = pl.program_id(0)

    @pl.when(s == 0)
    def _():
        proj_b[...] = jnp.zeros_like(proj_b)
        ym_a[...] = jnp.zeros_like(ym_a)

    @pl.when(jnp.logical_or(s == 0, lax.rem(s + n_t - 1, n_t) == 0))
    def _():
        xbuf_ref[0:8, :] = jnp.zeros((8, 2 * D_MLSTM), F32)
        state_ref[...] = jnp.zeros_like(state_ref)
        m_ref[...] = jnp.zeros_like(m_ref)

    R = dict(hn=hn_ref, w=w_ref, xo=xo_ref,
             gate=gate_ref, yr=yr_ref, wout=wout_ref, fg=fg_ref, convw=convw_ref, convb=convb_ref,
             gbias=gbias_ref, lnw=lnw_ref, skip=skip_ref, tril=tril_ref, o=o_ref, hres=hres_ref,
             xbuf=xbuf_ref, state=state_ref, m=m_ref)

    @pl.when(s % 2 == 0)
    def _():
        _mlstm_step(R, proj_b, proj_a, ym_b, ym_a, tt, final)

    @pl.when(s % 2 == 1)
    def _():
        _mlstm_step(R, proj_a, proj_b, ym_a, ym_b, tt, final)


def _mlstm_step(R, src_ref, dst_ref, ym_dst, ym_src, tt, final):
    L = MLSTM_CHUNK
    H = MLSTM_HEADS
    dh = MLSTM_HEAD
    d_model = R["xo"].shape[-1]
    wide = 2 * LANES

    y_cat = jnp.concatenate([R["yr"][0], ym_src[...]], axis=1)
    hn = R["hn"][0]

    def out_chunk(o):
        mix = _dot(y_cat, R["wout"][:, o:o + wide])
        R["hres"][:, o:o + wide] = R["xo"][0, :, o:o + wide] + R["gate"][0, :, o:o + wide] * mix

    def proj_chunk(o):
        w = min(wide, MLSTM_COLS - o)
        dst_ref[:, o:o + w] = _dot(hn, R["w"][:, o:o + w])

    pending = ([functools.partial(out_chunk, o) for o in range(0, d_model, wide)] +
               [functools.partial(proj_chunk, o) for o in range(0, MLSTM_COLS, wide)])

    def tick(n=1):
        for _ in range(n):
            if pending:
                pending.pop(0)()

    qk_parts = []
    for o in range(0, 2 * D_MLSTM, wide):
        cols = slice(o, o + wide)
        R["xbuf"][8:8 + tt, cols] = src_ref[:, M_QK + o:M_QK + o + wide]
        xb = R["xbuf"][:, cols]
        acc = R["convb"][:, cols] + R["convw"][CONV_K - 1:CONV_K, cols] * xb[8:8 + tt]
        for j in range(CONV_K - 1):
            sh = CONV_K - 1 - j
            acc = acc + R["convw"][j:j + 1, cols] * pltpu.roll(xb, sh, 0)[8:8 + tt]
        R["xbuf"][0:8, cols] = xb[tt:tt + 8]
        qk_parts.append(_silu(acc))
        tick()
    qk = jnp.concatenate(qk_parts, axis=1)
    q_c = qk[:, :D_MLSTM]
    k_c = qk[:, D_MLSTM:] * F32(dh ** -0.5)

    gt = src_ref[:, M_G:M_G + GATE_PAD] + R["gbias"][...]
    lane = lax.broadcasted_iota(jnp.int32, gt.shape, 1)
    is_f = (lane >= H) & (lane < 2 * H)
    log_f = jnp.minimum(gt, 0.0) - jnp.log(1.0 + jnp.exp(-jnp.abs(gt)))
    gates = jnp.where(is_f, log_f, gt)
    bcum = _dot_split_rhs(R["tril"][...], gates, 2)

    ti = lax.broadcasted_iota(jnp.int32, (L, L), 0)
    si = lax.broadcasted_iota(jnp.int32, (L, L), 1)
    causal = si <= ti
    ones_l = jnp.ones((L, dh), BF16)
    n_chunks = tt // L
    units = [(ci, h) for ci in range(n_chunks) for h in range(H)]
    rows_of = lambda ci: slice(ci * L, (ci + 1) * L)
    cols_of = lambda h: slice(h * dh, (h + 1) * dh)

    g_rows = [jnp.transpose(gates[rows_of(ci)]) for ci in range(n_chunks)]
    b_rows = [jnp.transpose(bcum[rows_of(ci)]) for ci in range(n_chunks)]
    q = [q_c[rows_of(ci), cols_of(h)] for ci, h in units]
    kh = [k_c[rows_of(ci), cols_of(h)] for ci, h in units]
    v_aug = [jnp.concatenate([src_ref[rows_of(ci), M_V + h * dh:M_V + (h + 1) * dh].astype(BF16),
                              ones_l], axis=1) for ci, h in units]
    b_col = [bcum[rows_of(ci), H + h:H + h + 1] for ci, h in units]
    i_col = [gates[rows_of(ci), h:h + 1] for ci, h in units]
    row_v = [g_rows[ci][h:h + 1, :] - b_rows[ci][H + h:H + h + 1, :] for ci, h in units]
    tick()
    d_log = [jnp.where(causal, bc + rv, -jnp.inf) for bc, rv in zip(b_col, row_v)]
    a_t = [jnp.max(d, axis=-1, keepdims=True) for d in d_log]
    tick()
    scores = []
    for q_, k_, d, a in zip(q, kh, d_log, a_t):
        scores.append(_dot_nt(q_.astype(BF16), k_.astype(BF16)) * jnp.exp(d - a))
        if len(scores) % 2 == 0:
            tick()
    b_last = [bc[L - 1:L, :] for bc in b_col]
    a_last = [a[L - 1:L, :] for a in a_t]
    kw = [(k_ * jnp.exp(bl - bc + ic - al)).astype(BF16)
          for k_, bl, bc, ic, al in zip(kh, b_last, b_col, i_col, a_last)]
    upd = [_dot_tn(kw_, va) for kw_, va in zip(kw, v_aug)]
    tick()

    st = [R["state"][h] for h in range(H)]
    m_prev = [R["m"][h:h + 1, 0:1] for h in range(H)]
    for ci in range(n_chunks):
        ids = [ci * H + h for h in range(H)]
        inter_log = [b_col[i] + m_prev[h] for h, i in enumerate(ids)]
        m_t = [jnp.maximum(il, a_t[i]) for il, i in zip(inter_log, ids)]
        nd = [_dot(jnp.concatenate([(jnp.exp(a_t[i] - mt) * scores[i]).astype(BF16),
                                    (jnp.exp(il - mt) * q[i]).astype(BF16)], axis=1),
                   jnp.concatenate([v_aug[i], st[h].astype(BF16)], axis=0))
              for h, (i, mt, il) in enumerate(zip(ids, m_t, inter_log))]
        hh = [x[:, :dh] / jnp.maximum(jnp.abs(x[:, dh:]), jnp.exp(-mt)) for x, mt in zip(nd, m_t)]
        m_new = [mt[L - 1:L, :] for mt in m_t]
        st = [jnp.exp(b_last[i] + mp - mn) * s_ + jnp.exp(a_last[i] - mn) * upd[i]
              for i, mp, mn, s_ in zip(ids, m_prev, m_new, st)]
        m_prev = m_new
        mean = [jnp.mean(x, axis=-1, keepdims=True) for x in hh]
        dv = [x - mu for x, mu in zip(hh, mean)]
        var = [jnp.mean(x * x, axis=-1, keepdims=True) for x in dv]
        for h in range(H):
            rows, cols = rows_of(ci), cols_of(h)
            hn_ = dv[h] * lax.rsqrt(var[h] + MLSTM_LN_EPS) * R["lnw"][:, cols]
            o_gate = _sigmoid(src_ref[rows, M_O + h * dh:M_O + (h + 1) * dh])
            out = o_gate * hn_ + R["skip"][:, cols] * q_c[rows, cols]
            z = src_ref[rows, M_Z + h * dh:M_Z + (h + 1) * dh]
            ym_dst[rows, cols] = (out * _silu(z)).astype(ym_dst.dtype)
            tick()
    for h in range(H):
        R["state"][h] = st[h]
        R["m"][h:h + 1, :] = jnp.broadcast_to(m_prev[h], (1, LANES))
    tick(len(pending))

    hres = R["hres"][...]
    if final:
        ms = jnp.mean(hres * hres, axis=-1, keepdims=True)
        hres = hres * lax.rsqrt(ms + NORM_EPS) * R["fg"][...]
    R["o"][0] = hres.astype(R["o"].dtype)


def _mlstm_out(h_res, hn, w_all, y_r, ada, w_out, final_gain, p, tt, final, out_dtype):
    bsz, t, d = h_res.shape
    n_t = t // tt
    n_tiles = bsz * n_t
    nxt = lambda s: jnp.minimum(s, n_tiles - 1)
    old = lambda s: jnp.clip(s - 2, 0, n_tiles - 1)
    c2 = lambda s: (0, 0)
    vec = lambda n: pl.BlockSpec((1, n), c2)
    v1 = lambda a: a.reshape(1, -1).astype(F32)
    gbias = jnp.concatenate([p["mlstm_b_i"], p["mlstm_b_f"],
                             jnp.zeros((GATE_PAD - 2 * MLSTM_HEADS,), F32)]).reshape(1, GATE_PAD)
    tile_of = lambda f, n: pl.BlockSpec((1, tt, n), lambda s: (f(s) // n_t, f(s) % n_t, 0))
    gate_of = lambda f: pl.BlockSpec((1, 1, d), lambda s: (f(s) // n_t, 0, 2))
    resident = lambda shape: pl.BlockSpec(shape, c2, pipeline_mode=pl.Buffered(1))
    return pl.pallas_call(
        functools.partial(_mlstm_kernel, tt=tt, n_t=n_t, final=final),
        grid=(n_tiles + 2,),
        in_specs=[tile_of(nxt, d),
                  pl.BlockSpec((d, MLSTM_COLS), lambda s: (0, 1), pipeline_mode=pl.Buffered(1)),
                  tile_of(old, d), gate_of(old), tile_of(old, D_RWKV),
                  resident((D_RWKV + D_MLSTM, d)), vec(d),
                  pl.BlockSpec((CONV_K, 2 * D_MLSTM), c2),
                  vec(2 * D_MLSTM), vec(GATE_PAD), vec(D_MLSTM), vec(D_MLSTM),
                  pl.BlockSpec((tt, tt), c2)],
        out_specs=tile_of(old, d),
        out_shape=jax.ShapeDtypeStruct((bsz, t, d), out_dtype),
        scratch_shapes=[pltpu.VMEM((tt, MLSTM_COLS), F32),
                        pltpu.VMEM((tt, MLSTM_COLS), F32),
                        pltpu.VMEM((tt, D_MLSTM), BF16),
                        pltpu.VMEM((tt, D_MLSTM), BF16),
                        pltpu.VMEM((tt, d), F32),
                        pltpu.VMEM((tt + 8, 2 * D_MLSTM), F32),
                        pltpu.VMEM((MLSTM_HEADS, MLSTM_HEAD, 2 * MLSTM_HEAD), F32),
                        pltpu.VMEM((8, LANES), F32)],
        compiler_params=pltpu.CompilerParams(dimension_semantics=("arbitrary",),
                                             vmem_limit_bytes=VMEM_LIMIT),
        name="mlstm_out",
    )(hn, w_all, h_res, ada.reshape(bsz, 1, 3 * d), y_r, w_out, final_gain.reshape(1, d),
      p["mlstm_conv_w"].astype(F32), v1(p["mlstm_conv_b"]), gbias,
      v1(p["mlstm_ln_w"]), v1(p["mlstm_skip"]), _chunk_tri(tt, MLSTM_CHUNK))


def _regroup_w_in(w):
    o = RWKV_COLS + 4 * D_MLSTM
    gates_end = o + 2 * MLSTM_HEADS
    zeros = lambda n: jnp.zeros((w.shape[0], n), w.dtype)
    return jnp.concatenate([w[:, :RWKV_COLS], zeros(MLSTM_COLS - RWKV_COLS),
                            w[:, RWKV_COLS:o], w[:, gates_end:], w[:, o:gates_end],
                            zeros(GATE_PAD - 2 * MLSTM_HEADS)], axis=1).astype(BF16)


def _tile(t, want):
    tile = min(t, want)
    assert t % tile == 0
    return tile


def kernel(x, c, w_ada, b_ada, norm_gain, w_in, mu_rwkv, w_decay_up, w_decay0, w_icl_up, a0, k_k, k_a, r_k, rwkv_gn_w, rwkv_gn_b, mlstm_conv_w, mlstm_conv_b, mlstm_b_i, mlstm_b_f, mlstm_ln_w, mlstm_skip, w_out, final_gain):
    bsz, t, d = x.shape
    depth = w_ada.shape[0]
    tt_r = _tile(t, 256)
    tt_m = _tile(t, 512)
    assert t % RWKV_CHUNK == 0 and t % MLSTM_CHUNK == 0
    h_res = x.astype(F32)
    c32 = c.astype(F32)
    for l in range(depth):
        ada = _ada(c32, w_ada[l], b_ada[l])
        w_all = _regroup_w_in(w_in[l])
        rp = dict(mu_rwkv=mu_rwkv[l], w_decay_up=w_decay_up[l], w_decay0=w_decay0[l],
                  w_icl_up=w_icl_up[l], a0=a0[l], k_k=k_k[l], k_a=k_a[l], r_k=r_k[l],
                  rwkv_gn_w=rwkv_gn_w[l], rwkv_gn_b=rwkv_gn_b[l])
        y_r, hn = _rwkv(h_res, ada, norm_gain[l], w_all, rp, tt_r, BF16)
        mp = dict(mlstm_conv_w=mlstm_conv_w[l], mlstm_conv_b=mlstm_conv_b[l], mlstm_b_i=mlstm_b_i[l],
                  mlstm_b_f=mlstm_b_f[l], mlstm_ln_w=mlstm_ln_w[l], mlstm_skip=mlstm_skip[l])
        final = l == depth - 1
        h_res = _mlstm_out(h_res, hn, w_all, y_r, ada,
                           w_out[l].astype(BF16), final_gain, mp, tt_m, final,
                           x.dtype if final else F32)
    return h_res
```

```python
import functools

import jax
import jax.numpy as jnp
from jax import lax
from jax.experimental import pallas as pl
from jax.experimental.pallas import tpu as pltpu

F32 = jnp.float32
BF16 = jnp.bfloat16

RWKV_HEAD = 64
RWKV_HEADS = 8
D_RWKV = RWKV_HEAD * RWKV_HEADS
MLSTM_HEADS = 4
MLSTM_HEAD = 128
D_MLSTM = MLSTM_HEADS * MLSTM_HEAD
LORA = 64
D_SHIFT = 3 * D_RWKV + 2 * LORA
CONV_K = 4
NORM_EPS = 1e-6
RWKV_GN_EPS = 64e-5
MLSTM_LN_EPS = 1e-6
GATE_PAD = 128

RWKV_CHUNK = 64
MLSTM_CHUNK = 128
LANES = 128
VMEM_LIMIT = 56 * 1024 * 1024


def _dot(a, b):
    return jnp.dot(a, b, preferred_element_type=F32)


def _dot_nt(a, b):
    return lax.dot_general(a, b, (((1,), (1,)), ((), ())), preferred_element_type=F32)


def _dot_tn(a, b):
    return lax.dot_general(a, b, (((0,), (0,)), ((), ())), preferred_element_type=F32)


def _split_bf16(x, n):
    parts = []
    rem = x
    for i in range(n):
        p = rem.astype(BF16)
        parts.append(p)
        if i + 1 < n:
            rem = rem - p.astype(F32)
    return parts


def _dot_split_rhs(m, x, n):
    acc = None
    for p in _split_bf16(x, n):
        d = _dot(m, p)
        acc = d if acc is None else acc + d
    return acc


def _sigmoid(x):
    return 1.0 / (1.0 + jnp.exp(-x))


def _silu(x):
    return x * _sigmoid(x)


def _ada_kernel(c_ref, w_ref, b_ref, o_ref):
    ca = _silu(c_ref[...])
    c_hi, c_lo = _split_bf16(ca, 2)
    w = w_ref[...]
    w_hi, w_lo = _split_bf16(w, 2)
    o_ref[...] = _dot(c_hi, w_hi) + _dot(c_hi, w_lo) + _dot(c_lo, w_hi) + b_ref[...]


def _ada(c, w, b):
    bsz, d = c.shape
    n = w.shape[1]
    bn = d
    return pl.pallas_call(
        _ada_kernel,
        grid=(n // bn,),
        in_specs=[pl.BlockSpec((bsz, d), lambda j: (0, 0)),
                  pl.BlockSpec((d, bn), lambda j: (0, j)),
                  pl.BlockSpec((1, bn), lambda j: (0, j))],
        out_specs=pl.BlockSpec((bsz, bn), lambda j: (0, j)),
        out_shape=jax.ShapeDtypeStruct((bsz, n), F32),
        compiler_params=pltpu.CompilerParams(dimension_semantics=("arbitrary",),
                                             vmem_limit_bytes=VMEM_LIMIT),
        name="ada",
    )(c, w, b.reshape(1, n))


RWKV_COLS = D_SHIFT + D_RWKV
MLSTM_COLS = 5 * D_MLSTM + GATE_PAD


def _adaln_norm(x, scale, shift, gain):
    ms = jnp.mean(x * x, axis=-1, keepdims=True)
    return x * lax.rsqrt(ms + NORM_EPS) * (gain * (1.0 + scale)) + shift


def _pair_blockdiag(x):
    lane = lax.broadcasted_iota(jnp.int32, x.shape, 1)
    first = lane < RWKV_HEAD
    zero = jnp.zeros_like(x)
    return jnp.concatenate([jnp.where(first, x, zero), jnp.where(first, zero, x)], axis=0)


def _rwkv_kernel(x_ref, scale_ref, shift_ref, gain_ref, w_ref, mu_ref, lora_ref, w0_ref, a0_ref,
                 kk_ref, ka_ref, rk_ref, gnw_ref, gnb_ref, hsum_ref, tril_ref,
                 y_ref, hn_ref,
                 proj_a, proj_b, carry_ref, state_ref, at_ref, rt_ref, bt_ref, kt_ref, bp_ref, kp_ref,
                 vb_ref, gam_ref, yacc_ref, bonus_ref, *, tt, n_t):
    s = pl.program_id(0)

    @pl.when(s == 0)
    def _():
        proj_b[...] = jnp.zeros_like(proj_b)

    @pl.when(jnp.logical_or(s == 0, lax.rem(s + n_t - 1, n_t) == 0))
    def _():
        carry_ref[...] = jnp.zeros_like(carry_ref)
        state_ref[...] = jnp.zeros_like(state_ref)

    refs = dict(x=x_ref, scale=scale_ref, shift=shift_ref, gain=gain_ref, w=w_ref, mu=mu_ref,
                lora=lora_ref, w0=w0_ref, a0=a0_ref, kk=kk_ref, ka=ka_ref, rk=rk_ref, gnw=gnw_ref,
                gnb=gnb_ref, hsum=hsum_ref, tril=tril_ref, y=y_ref, hn=hn_ref, carry=carry_ref,
                state=state_ref,
                at=at_ref, rt=rt_ref, bt=bt_ref, kt=kt_ref, bp=bp_ref, kp=kp_ref, vb=vb_ref,
                gam=gam_ref, yacc=yacc_ref, bonus=bonus_ref)

    @pl.when(s % 2 == 0)
    def _():
        _rwkv_step(refs, proj_b, proj_a, tt)

    @pl.when(s % 2 == 1)
    def _():
        _rwkv_step(refs, proj_a, proj_b, tt)


def _rwkv_step(R, src_ref, dst_ref, tt):
    L = RWKV_CHUNK
    n_pairs = D_RWKV // LANES
    n_chunks = tt // L

    hn = _adaln_norm(R["x"][0], R["scale"][0], R["shift"][0], R["gain"][...]).astype(BF16)
    R["hn"][0] = hn
    pending = [(o, min(2 * LANES, RWKV_COLS - o)) for o in range(0, RWKV_COLS, 2 * LANES)]

    def tick(n=1):
        for _ in range(n):
            if pending:
                o, w = pending.pop(0)
                dst_ref[:, o:o + w] = _dot(hn, R["w"][:, o:o + w])

    row0 = lax.broadcasted_iota(jnp.int32, (tt, LANES), 0) == 0

    def shifted(g):
        cols = slice(g * LANES, (g + 1) * LANES)
        raw = src_ref[:, cols]
        prev = jnp.where(row0, R["carry"][:, cols], pltpu.roll(raw, 1, 0))
        R["carry"][:, cols] = raw[tt - 1:tt, :]
        return raw + R["mu"][:, cols] * (prev - raw)

    g_lora = shifted(3 * n_pairs)
    lw = _dot(jnp.tanh(g_lora).astype(BF16), R["lora"][0])
    la = _dot(g_lora.astype(BF16), R["lora"][1])
    hsum = R["hsum"][...]
    tick()
    for p in range(n_pairs):
        cols = slice(p * LANES, (p + 1) * LANES)
        r = shifted(p)
        k = shifted(n_pairs + p)
        v = shifted(2 * n_pairs + p)
        ld = -jnp.exp(F32(-0.5)) * _sigmoid(R["w0"][:, cols] + lw[:, cols])
        a = _sigmoid(R["a0"][:, cols] + la[:, cols])
        kk = k * R["kk"][:, cols]
        k2 = k * (1.0 + (a - 1.0) * R["ka"][:, cols])
        sums = _dot(jnp.concatenate([kk * kk, r * k2 * R["rk"][:, cols]], axis=1).astype(BF16), hsum)
        kk = kk / jnp.maximum(jnp.sqrt(sums[:, :LANES]), 1e-12)
        R["bonus"][:, cols] = sums[:, LANES:] * v
        kka = kk * a
        c2 = _dot(R["tril"][...], jnp.concatenate(_split_bf16(ld, 2), axis=1))
        c = c2[:, :LANES] + c2[:, LANES:]
        R["gam"][:, cols] = c
        c_last = jnp.concatenate(
            [jnp.broadcast_to(R["gam"][ci * L + L - 1:ci * L + L, cols], (L, LANES))
             for ci in range(n_chunks)], axis=0)
        gam = jnp.exp(c_last)
        e_pos = jnp.exp(c)
        e_neg = 1.0 / e_pos
        e_rev = gam * e_neg
        R["rt"][:, cols] = (r * e_pos).astype(BF16)
        R["kt"][:, cols] = (k2 * e_neg).astype(BF16)
        R["at"][:, cols] = (-kk * jnp.exp(c - ld)).astype(BF16)
        R["bt"][:, cols] = (kka * e_neg).astype(BF16)
        R["kp"][:, cols] = (k2 * e_rev).astype(BF16)
        R["bp"][:, cols] = (kka * e_rev).astype(BF16)
        R["vb"][:, cols] = v.astype(BF16)
        R["gam"][:, cols] = gam
        tick(2)

    ti = lax.broadcasted_iota(jnp.int32, (L, LANES), 0)
    si = lax.broadcasted_iota(jnp.int32, (L, LANES), 1) % L
    strict = si < ti
    incl = si <= ti
    eye_pair = (si == ti).astype(F32)
    xr = ti ^ si
    r2 = lax.broadcasted_iota(jnp.int32, (LANES, LANES), 0)
    c2 = lax.broadcasted_iota(jnp.int32, (LANES, LANES), 1)
    same_head = (r2 < RWKV_HEAD) == (c2 < RWKV_HEAD)
    eye_full = r2 == c2

    chains = [(ci, p) for p in range(n_pairs) for ci in range(n_chunks)]

    def load(name):
        return [R[name][ci * L:(ci + 1) * L, p * LANES:(p + 1) * LANES] for ci, p in chains]

    at, rt, bt, kt, bp, kp, vb = (load(n) for n in ("at", "rt", "bt", "kt", "bp", "kp", "vb"))
    zero = jnp.zeros((L, LANES), F32)
    zero2 = jnp.zeros((LANES, LANES), F32)
    aa = [_dot_nt(jnp.concatenate([a_, r_], axis=0),
                  jnp.concatenate([_pair_blockdiag(b_), _pair_blockdiag(k_)], axis=0))
          for a_, r_, b_, k_ in zip(at, rt, bt, kt)]
    ab = [jnp.where(strict, x[:L, :LANES], zero) for x in aa]
    ak = [jnp.where(strict, x[:L, LANES:], zero).astype(BF16) for x in aa]
    rb = [jnp.where(incl, x[L:, :LANES], zero).astype(BF16) for x in aa]
    rk = [jnp.where(incl, x[L:, LANES:], zero).astype(BF16) for x in aa]
    tinv = [eye_pair + jnp.where((xr >> 1) == 0, x, zero) for x in ab]
    for lvl in range(1, 6):
        yy = [_dot(jnp.where((xr >> lvl) == 1, x, zero).astype(BF16), _pair_blockdiag(t.astype(BF16)))
              for x, t in zip(ab, tinv)]
        tinv = [t + _dot(t.astype(BF16), _pair_blockdiag(y.astype(BF16))) for t, y in zip(tinv, yy)]
    wv = [_dot(jnp.concatenate([x, z_], axis=0), _pair_blockdiag(y)) for x, z_, y in zip(ak, rk, vb)]
    w = [x[:L].astype(BF16) for x in wv]
    ap = [_dot(t.astype(BF16), jnp.concatenate([_pair_blockdiag(a_), _pair_blockdiag(w_)], axis=1))
          for t, a_, w_ in zip(tinv, at, w)]
    ahat = [x[:, :LANES].astype(BF16) for x in ap]
    p1 = [x[:, LANES:].astype(BF16) for x in ap]
    q = [_dot(x, jnp.concatenate([_pair_blockdiag(a_), _pair_blockdiag(p_)], axis=1))
         for x, a_, p_ in zip(rb, ahat, p1)]
    y_in = [q_[:, LANES:] + x[L:] for q_, x in zip(q, wv)]
    mg = [_dot_tn(jnp.concatenate([b_, k_], axis=0),
                  jnp.concatenate([jnp.concatenate([a_, p_], axis=1),
                                   jnp.concatenate([jnp.zeros_like(v_), v_], axis=1)], axis=0))
          for b_, k_, a_, p_, v_ in zip(bp, kp, ahat, p1, vb)]
    lhs = [jnp.concatenate([r_.astype(F32) + q_[:, :LANES], jnp.where(same_head, m_[:, :LANES], zero2)],
                           axis=0).astype(BF16) for r_, q_, m_ in zip(rt, q, mg)]
    gg = [jnp.where(same_head, m_[:, LANES:], zero2) for m_ in mg]
    gam_col = [jnp.sum(jnp.where(eye_full,
                                 jnp.broadcast_to(R["gam"][ci * L:ci * L + 1, p * LANES:(p + 1) * LANES],
                                                  (LANES, LANES)), zero2), axis=1, keepdims=True)
               for ci, p in chains]

    hs = [R["state"][p] for p in range(n_pairs)]
    for ci in range(n_chunks):
        idx = [p * n_chunks + ci for p in range(n_pairs)]
        yh = [_dot(lhs[i], h.astype(BF16)) for i, h in zip(idx, hs)]
        for p, i in enumerate(idx):
            R["yacc"][ci * L:(ci + 1) * L, p * LANES:(p + 1) * LANES] = yh[p][:L] + y_in[i]
        hs = [gam_col[i] * h + y_[L:] + gg[i] for i, h, y_ in zip(idx, hs, yh)]
    for p in range(n_pairs):
        R["state"][p] = hs[p]

    inv_n = F32(1.0 / RWKV_HEAD)
    for o in range(0, D_RWKV, 2 * LANES):
        cols = slice(o, o + 2 * LANES)
        y = R["yacc"][:, cols]
        mean = _dot(y.astype(BF16), hsum) * inv_n
        d = y - mean
        var = _dot((d * d).astype(BF16), hsum) * inv_n
        yn = d * lax.rsqrt(var + RWKV_GN_EPS) * R["gnw"][:, cols] + R["gnb"][:, cols]
        z = src_ref[:, D_SHIFT + o:D_SHIFT + o + 2 * LANES]
        R["y"][0, :, cols] = ((yn + R["bonus"][:, cols]) * _silu(z)).astype(R["y"].dtype)
        tick(2)
    tick(len(pending))


def _chunk_tri(n, chunk):
    i = jnp.arange(n)[:, None]
    j = jnp.arange(n)[None, :]
    return (((i // chunk) == (j // chunk)) & (j <= i)).astype(BF16)


def _rwkv(h_res, ada, gain, w_all, p, tt, out_dtype):
    bsz, t, d = h_res.shape
    n_t = t // tt
    n_tiles = bsz * n_t
    nxt = lambda s: jnp.minimum(s, n_tiles - 1)
    cur = lambda s: jnp.maximum(s - 1, 0)
    c2 = lambda s: (0, 0)
    vec = lambda n: pl.BlockSpec((1, n), c2)
    hsum = ((jnp.arange(2 * LANES)[:, None] // RWKV_HEAD) ==
            (jnp.arange(2 * LANES)[None, :] // RWKV_HEAD)).astype(BF16)
    lora = jnp.stack([
        jnp.concatenate([p["w_decay_up"], jnp.zeros((LORA, D_RWKV), F32)], axis=0),
        jnp.concatenate([jnp.zeros((LORA, D_RWKV), F32), p["w_icl_up"]], axis=0)]).astype(BF16)
    v1 = lambda a: a.reshape(1, -1).astype(F32)
    act = lambda: pltpu.VMEM((tt, D_RWKV), BF16)
    f32_tile = lambda: pltpu.VMEM((tt, D_RWKV), F32)
    return pl.pallas_call(
        functools.partial(_rwkv_kernel, tt=tt, n_t=n_t),
        grid=(n_tiles + 1,),
        in_specs=[pl.BlockSpec((1, tt, d), lambda s: (nxt(s) // n_t, nxt(s) % n_t, 0)),
                  pl.BlockSpec((1, 1, d), lambda s: (nxt(s) // n_t, 0, 1)),
                  pl.BlockSpec((1, 1, d), lambda s: (nxt(s) // n_t, 0, 0)),
                  vec(d),
                  pl.BlockSpec((d, MLSTM_COLS), c2, pipeline_mode=pl.Buffered(1)),
                  vec(D_SHIFT),
                  pl.BlockSpec((2, 2 * LORA, D_RWKV), lambda s: (0, 0, 0)),
                  vec(D_RWKV), vec(D_RWKV), vec(D_RWKV), vec(D_RWKV), vec(D_RWKV),
                  vec(D_RWKV), vec(D_RWKV),
                  pl.BlockSpec((2 * LANES, 2 * LANES), c2),
                  pl.BlockSpec((tt, tt), c2)],
        out_specs=[pl.BlockSpec((1, tt, D_RWKV), lambda s: (cur(s) // n_t, cur(s) % n_t, 0)),
                   pl.BlockSpec((1, tt, d), lambda s: (nxt(s) // n_t, nxt(s) % n_t, 0))],
        out_shape=[jax.ShapeDtypeStruct((bsz, t, D_RWKV), out_dtype),
                   jax.ShapeDtypeStruct((bsz, t, d), BF16)],
        scratch_shapes=[pltpu.VMEM((tt, RWKV_COLS), F32),
                        pltpu.VMEM((tt, RWKV_COLS), F32),
                        pltpu.VMEM((1, D_SHIFT), F32),
                        pltpu.VMEM((D_RWKV // LANES, LANES, LANES), F32),
                        act(), act(), act(), act(), act(), act(), act(),
                        f32_tile(), f32_tile(), f32_tile()],
        compiler_params=pltpu.CompilerParams(dimension_semantics=("arbitrary",),
                                             vmem_limit_bytes=VMEM_LIMIT),
        name="rwkv7",
    )(h_res, ada.reshape(bsz, 1, 3 * d), ada.reshape(bsz, 1, 3 * d), gain.reshape(1, d), w_all,
      v1(p["mu_rwkv"]), lora, v1(p["w_decay0"]), v1(p["a0"]), v1(p["k_k"]), v1(p["k_a"]),
      v1(p["r_k"]), v1(p["rwkv_gn_w"]), v1(p["rwkv_gn_b"]), hsum, _chunk_tri(tt, RWKV_CHUNK))


M_QK, M_V, M_O, M_Z, M_G = 0, 2 * D_MLSTM, 3 * D_MLSTM, 4 * D_MLSTM, 5 * D_MLSTM


def _mlstm_kernel(hn_ref, w_ref,
                  xo_ref, gate_ref, yr_ref, wout_ref, fg_ref,
                  convw_ref, convb_ref, gbias_ref, lnw_ref, skip_ref, tril_ref,
                  o_ref,
                  proj_a, proj_b, ym_a, ym_b, hres_ref, xbuf_ref, state_ref, m_ref, *, tt, n_t, final):
    s = pl.program_id(0)

    @pl.when(s == 0)
    def _():
        proj_b[...] = jnp.zeros_like(proj_b)
        ym_a[...] = jnp.zeros_like(ym_a)

    @pl.when(jnp.logical_or(s == 0, lax.rem(s + n_t - 1, n_t) == 0))
    def _():
        xbuf_ref[0:8, :] = jnp.zeros((8, 2 * D_MLSTM), F32)
        state_ref[...] = jnp.zeros_like(state_ref)
        m_ref[...] = jnp.zeros_like(m_ref)

    R = dict(hn=hn_ref, w=w_ref, xo=xo_ref,
             gate=gate_ref, yr=yr_ref, wout=wout_ref, fg=fg_ref, convw=convw_ref, convb=convb_ref,
             gbias=gbias_ref, lnw=lnw_ref, skip=skip_ref, tril=tril_ref, o=o_ref, hres=hres_ref,
             xbuf=xbuf_ref, state=state_ref, m=m_ref)

    @pl.when(s % 2 == 0)
    def _():
        _mlstm_step(R, proj_b, proj_a, ym_b, ym_a, tt, final)

    @pl.when(s % 2 == 1)
    def _():
        _mlstm_step(R, proj_a, proj_b, ym_a, ym_b, tt, final)


def _mlstm_step(R, src_ref, dst_ref, ym_dst, ym_src, tt, final):
    L = MLSTM_CHUNK
    H = MLSTM_HEADS
    dh = MLSTM_HEAD
    d_model = R["xo"].shape[-1]
    wide = 2 * LANES

    y_cat = jnp.concatenate([R["yr"][0], ym_src[...]], axis=1)
    hn = R["hn"][0]

    def out_chunk(o):
        mix = _dot(y_cat, R["wout"][:, o:o + wide])
        R["hres"][:, o:o + wide] = R["xo"][0, :, o:o + wide] + R["gate"][0, :, o:o + wide] * mix

    def proj_chunk(o):
        w = min(wide, MLSTM_COLS - o)
        dst_ref[:, o:o + w] = _dot(hn, R["w"][:, o:o + w])

    pending = ([functools.partial(out_chunk, o) for o in range(0, d_model, wide)] +
               [functools.partial(proj_chunk, o) for o in range(0, MLSTM_COLS, wide)])

    def tick(n=1):
        for _ in range(n):
            if pending:
                pending.pop(0)()

    qk_parts = []
    for o in range(0, 2 * D_MLSTM, wide):
        cols = slice(o, o + wide)
        R["xbuf"][8:8 + tt, cols] = src_ref[:, M_QK + o:M_QK + o + wide]
        xb = R["xbuf"][:, cols]
        acc = R["convb"][:, cols] + R["convw"][CONV_K - 1:CONV_K, cols] * xb[8:8 + tt]
        for j in range(CONV_K - 1):
            sh = CONV_K - 1 - j
            acc = acc + R["convw"][j:j + 1, cols] * pltpu.roll(xb, sh, 0)[8:8 + tt]
        R["xbuf"][0:8, cols] = xb[tt:tt + 8]
        qk_parts.append(_silu(acc))
        tick()
    qk = jnp.concatenate(qk_parts, axis=1)
    q_c = qk[:, :D_MLSTM]
    k_c = qk[:, D_MLSTM:] * F32(dh ** -0.5)

    gt = src_ref[:, M_G:M_G + GATE_PAD] + R["gbias"][...]
    lane = lax.broadcasted_iota(jnp.int32, gt.shape, 1)
    is_f = (lane >= H) & (lane < 2 * H)
    log_f = jnp.minimum(gt, 0.0) - jnp.log(1.0 + jnp.exp(-jnp.abs(gt)))
    gates = jnp.where(is_f, log_f, gt)
    bcum = _dot_split_rhs(R["tril"][...], gates, 2)

    ti = lax.broadcasted_iota(jnp.int32, (L, L), 0)
    si = lax.broadcasted_iota(jnp.int32, (L, L), 1)
    causal = si <= ti
    ones_l = jnp.ones((L, dh), BF16)
    n_chunks = tt // L
    units = [(ci, h) for ci in range(n_chunks) for h in range(H)]
    rows_of = lambda ci: slice(ci * L, (ci + 1) * L)
    cols_of = lambda h: slice(h * dh, (h + 1) * dh)

    g_rows = [jnp.transpose(gates[rows_of(ci)]) for ci in range(n_chunks)]
    b_rows = [jnp.transpose(bcum[rows_of(ci)]) for ci in range(n_chunks)]
    q = [q_c[rows_of(ci), cols_of(h)] for ci, h in units]
    kh = [k_c[rows_of(ci), cols_of(h)] for ci, h in units]
    v_aug = [jnp.concatenate([src_ref[rows_of(ci), M_V + h * dh:M_V + (h + 1) * dh].astype(BF16),
                              ones_l], axis=1) for ci, h in units]
    b_col = [bcum[rows_of(ci), H + h:H + h + 1] for ci, h in units]
    i_col = [gates[rows_of(ci), h:h + 1] for ci, h in units]
    row_v = [g_rows[ci][h:h + 1, :] - b_rows[ci][H + h:H + h + 1, :] for ci, h in units]
    tick()
    d_log = [jnp.where(causal, bc + rv, -jnp.inf) for bc, rv in zip(b_col, row_v)]
    a_t = [jnp.max(d, axis=-1, keepdims=True) for d in d_log]
    tick()
    scores = []
    for q_, k_, d, a in zip(q, kh, d_log, a_t):
        scores.append(_dot_nt(q_.astype(BF16), k_.astype(BF16)) * jnp.exp(d - a))
        if len(scores) % 2 == 0:
            tick()
    b_last = [bc[L - 1:L, :] for bc in b_col]
    a_last = [a[L - 1:L, :] for a in a_t]
    kw = [(k_ * jnp.exp(bl - bc + ic - al)).astype(BF16)
          for k_, bl, bc, ic, al in zip(kh, b_last, b_col, i_col, a_last)]
    upd = [_dot_tn(kw_, va) for kw_, va in zip(kw, v_aug)]
    tick()

    st = [R["state"][h] for h in range(H)]
    m_prev = [R["m"][h:h + 1, 0:1] for h in range(H)]
    for ci in range(n_chunks):
        ids = [ci * H + h for h in range(H)]
        inter_log = [b_col[i] + m_prev[h] for h, i in enumerate(ids)]
        m_t = [jnp.maximum(il, a_t[i]) for il, i in zip(inter_log, ids)]
        nd = [_dot(jnp.concatenate([(jnp.exp(a_t[i] - mt) * scores[i]).astype(BF16),
                                    (jnp.exp(il - mt) * q[i]).astype(BF16)], axis=1),
                   jnp.concatenate([v_aug[i], st[h].astype(BF16)], axis=0))
              for h, (i, mt, il) in enumerate(zip(ids, m_t, inter_log))]
        hh = [x[:, :dh] / jnp.maximum(jnp.abs(x[:, dh:]), jnp.exp(-mt)) for x, mt in zip(nd, m_t)]
        m_new = [mt[L - 1:L, :] for mt in m_t]
        st = [jnp.exp(b_last[i] + mp - mn) * s_ + jnp.exp(a_last[i] - mn) * upd[i]
              for i, mp, mn, s_ in zip(ids, m_prev, m_new, st)]
        m_prev = m_new
        mean = [jnp.mean(x, axis=-1, keepdims=True) for x in hh]
        dv = [x - mu for x, mu in zip(hh, mean)]
        var = [jnp.mean(x * x, axis=-1, keepdims=True) for x in dv]
        for h in range(H):
            rows, cols = rows_of(ci), cols_of(h)
            hn_ = dv[h] * lax.rsqrt(var[h] + MLSTM_LN_EPS) * R["lnw"][:, cols]
            o_gate = _sigmoid(src_ref[rows, M_O + h * dh:M_O + (h + 1) * dh])
            out = o_gate * hn_ + R["skip"][:, cols] * q_c[rows, cols]
            z = src_ref[rows, M_Z + h * dh:M_Z + (h + 1) * dh]
            ym_dst[rows, cols] = (out * _silu(z)).astype(ym_dst.dtype)
            tick()
    for h in range(H):
        R["state"][h] = st[h]
        R["m"][h:h + 1, :] = jnp.broadcast_to(m_prev[h], (1, LANES))
    tick(len(pending))

    hres = R["hres"][...]
    if final:
        ms = jnp.mean(hres * hres, axis=-1, keepdims=True)
        hres = hres * lax.rsqrt(ms + NORM_EPS) * R["fg"][...]
    R["o"][0] = hres.astype(R["o"].dtype)


def _mlstm_out(h_res, hn, w_all, y_r, ada, w_out, final_gain, p, tt, final, out_dtype):
    bsz, t, d = h_res.shape
    n_t = t // tt
    n_tiles = bsz * n_t
    nxt = lambda s: jnp.minimum(s, n_tiles - 1)
    old = lambda s: jnp.clip(s - 2, 0, n_tiles - 1)
    c2 = lambda s: (0, 0)
    vec = lambda n: pl.BlockSpec((1, n), c2)
    v1 = lambda a: a.reshape(1, -1).astype(F32)
    gbias = jnp.concatenate([p["mlstm_b_i"], p["mlstm_b_f"],
                             jnp.zeros((GATE_PAD - 2 * MLSTM_HEADS,), F32)]).reshape(1, GATE_PAD)
    tile_of = lambda f, n: pl.BlockSpec((1, tt, n), lambda s: (f(s) // n_t, f(s) % n_t, 0))
    gate_of = lambda f: pl.BlockSpec((1, 1, d), lambda s: (f(s) // n_t, 0, 2))
    resident = lambda shape: pl.BlockSpec(shape, c2, pipeline_mode=pl.Buffered(1))
    return pl.pallas_call(
        functools.partial(_mlstm_kernel, tt=tt, n_t=n_t, final=final),
        grid=(n_tiles + 2,),
        in_specs=[tile_of(nxt, d),
                  pl.BlockSpec((d, MLSTM_COLS), lambda s: (0, 1), pipeline_mode=pl.Buffered(1)),
                  tile_of(old, d), gate_of(old), tile_of(old, D_RWKV),
                  resident((D_RWKV + D_MLSTM, d)), vec(d),
                  pl.BlockSpec((CONV_K, 2 * D_MLSTM), c2),
                  vec(2 * D_MLSTM), vec(GATE_PAD), vec(D_MLSTM), vec(D_MLSTM),
                  pl.BlockSpec((tt, tt), c2)],
        out_specs=tile_of(old, d),
        out_shape=jax.ShapeDtypeStruct((bsz, t, d), out_dtype),
        scratch_shapes=[pltpu.VMEM((tt, MLSTM_COLS), F32),
                        pltpu.VMEM((tt, MLSTM_COLS), F32),
                        pltpu.VMEM((tt, D_MLSTM), BF16),
                        pltpu.VMEM((tt, D_MLSTM), BF16),
                        pltpu.VMEM((tt, d), F32),
                        pltpu.VMEM((tt + 8, 2 * D_MLSTM), F32),
                        pltpu.VMEM((MLSTM_HEADS, MLSTM_HEAD, 2 * MLSTM_HEAD), F32),
                        pltpu.VMEM((8, LANES), F32)],
        compiler_params=pltpu.CompilerParams(dimension_semantics=("arbitrary",),
                                             vmem_limit_bytes=VMEM_LIMIT),
        name="mlstm_out",
    )(hn, w_all, h_res, ada.reshape(bsz, 1, 3 * d), y_r, w_out, final_gain.reshape(1, d),
      p["mlstm_conv_w"].astype(F32), v1(p["mlstm_conv_b"]), gbias,
      v1(p["mlstm_ln_w"]), v1(p["mlstm_skip"]), _chunk_tri(tt, MLSTM_CHUNK))


def _regroup_w_in(w):
    o = RWKV_COLS + 4 * D_MLSTM
    gates_end = o + 2 * MLSTM_HEADS
    zeros = lambda n: jnp.zeros((w.shape[0], n), w.dtype)
    return jnp.concatenate([w[:, :RWKV_COLS], zeros(MLSTM_COLS - RWKV_COLS),
                            w[:, RWKV_COLS:o], w[:, gates_end:], w[:, o:gates_end],
                            zeros(GATE_PAD - 2 * MLSTM_HEADS)], axis=1).astype(BF16)


def _tile(t, want):
    tile = min(t, want)
    assert t % tile == 0
    return tile


def kernel(x, c, w_ada, b_ada, norm_gain, w_in, mu_rwkv, w_decay_up, w_decay0, w_icl_up, a0, k_k, k_a, r_k, rwkv_gn_w, rwkv_gn_b, mlstm_conv_w, mlstm_conv_b, mlstm_b_i, mlstm_b_f, mlstm_ln_w, mlstm_skip, w_out, final_gain):
    bsz, t, d = x.shape
    depth = w_ada.shape[0]
    tt_r = _tile(t, 512)
    tt_m = _tile(t, 512)
    assert t % RWKV_CHUNK == 0 and t % MLSTM_CHUNK == 0
    h_res = x.astype(F32)
    c32 = c.astype(F32)
    for l in range(depth):
        ada = _ada(c32, w_ada[l], b_ada[l])
        w_all = _regroup_w_in(w_in[l])
        rp = dict(mu_rwkv=mu_rwkv[l], w_decay_up=w_decay_up[l], w_decay0=w_decay0[l],
                  w_icl_up=w_icl_up[l], a0=a0[l], k_k=k_k[l], k_a=k_a[l], r_k=r_k[l],
                  rwkv_gn_w=rwkv_gn_w[l], rwkv_gn_b=rwkv_gn_b[l])
        y_r, hn = _rwkv(h_res, ada, norm_gain[l], w_all, rp, tt_r, BF16)
        mp = dict(mlstm_conv_w=mlstm_conv_w[l], mlstm_conv_b=mlstm_conv_b[l], mlstm_b_i=mlstm_b_i[l],
                  mlstm_b_f=mlstm_b_f[l], mlstm_ln_w=mlstm_ln_w[l], mlstm_skip=mlstm_skip[l])
        final = l == depth - 1
        h_res = _mlstm_out(h_res, hn, w_all, y_r, ada,
                           w_out[l].astype(BF16), final_gain, mp, tt_m, final,
                           x.dtype if final else F32)
    return h_res
```

```python
import functools

import jax
import jax.numpy as jnp
from jax import lax
from jax.experimental import pallas as pl
from jax.experimental.pallas import tpu as pltpu

F32 = jnp.float32
BF16 = jnp.bfloat16

RWKV_HEAD = 64
RWKV_HEADS = 8
D_RWKV = RWKV_HEAD * RWKV_HEADS
MLSTM_HEADS = 4
MLSTM_HEAD = 128
D_MLSTM = MLSTM_HEADS * MLSTM_HEAD
LORA = 64
D_SHIFT = 3 * D_RWKV + 2 * LORA
CONV_K = 4
NORM_EPS = 1e-6
RWKV_GN_EPS = 64e-5
MLSTM_LN_EPS = 1e-6
GATE_PAD = 128

RWKV_CHUNK = 64
MLSTM_CHUNK = 128
LANES = 128
SUBLANES = 8
RWKV_TILE = 256
MLSTM_TILE = 512
VMEM_LIMIT = 56 * 1024 * 1024


def _dot(a, b):
    return jnp.dot(a, b, preferred_element_type=F32)


def _dot_nt(a, b):
    return lax.dot_general(a, b, (((1,), (1,)), ((), ())), preferred_element_type=F32)


def _dot_tn(a, b):
    return lax.dot_general(a, b, (((0,), (0,)), ((), ())), preferred_element_type=F32)


def _split_bf16(x, n):
    parts = []
    rem = x
    for i in range(n):
        p = rem.astype(BF16)
        parts.append(p)
        if i + 1 < n:
            rem = rem - p.astype(F32)
    return parts


def _dot_split_rhs(m, x, n):
    acc = None
    for p in _split_bf16(x, n):
        d = _dot(m, p)
        acc = d if acc is None else acc + d
    return acc


def _sigmoid(x):
    return 1.0 / (1.0 + jnp.exp(-x))


def _silu(x):
    return x * _sigmoid(x)


def _ada_kernel(c_ref, w_ref, b_ref, o_ref):
    ca = _silu(c_ref[...])
    c_hi, c_lo = _split_bf16(ca, 2)
    w = w_ref[...]
    w_hi, w_lo = _split_bf16(w, 2)
    o_ref[...] = _dot(c_hi, w_hi) + _dot(c_hi, w_lo) + _dot(c_lo, w_hi) + b_ref[...]


def _ada(c, w, b):
    bsz, d = c.shape
    n = w.shape[1]
    bn = d
    return pl.pallas_call(
        _ada_kernel,
        grid=(n // bn,),
        in_specs=[pl.BlockSpec((bsz, d), lambda j: (0, 0)),
                  pl.BlockSpec((d, bn), lambda j: (0, j)),
                  pl.BlockSpec((1, bn), lambda j: (0, j))],
        out_specs=pl.BlockSpec((bsz, bn), lambda j: (0, j)),
        out_shape=jax.ShapeDtypeStruct((bsz, n), F32),
        compiler_params=pltpu.CompilerParams(dimension_semantics=("arbitrary",),
                                             vmem_limit_bytes=VMEM_LIMIT),
        name="ada",
    )(c, w, b.reshape(1, n))


RWKV_COLS = D_SHIFT + D_RWKV
MLSTM_COLS = 5 * D_MLSTM + GATE_PAD


def _adaln_norm(x, scale, shift, gain):
    ms = jnp.mean(x * x, axis=-1, keepdims=True)
    return x * lax.rsqrt(ms + NORM_EPS) * (gain * (1.0 + scale)) + shift


def _pair_blockdiag(x):
    lane = lax.broadcasted_iota(jnp.int32, x.shape, 1)
    first = lane < RWKV_HEAD
    zero = jnp.zeros_like(x)
    return jnp.concatenate([jnp.where(first, x, zero), jnp.where(first, zero, x)], axis=0)


def _rwkv_kernel(x_ref, scale_ref, shift_ref, gain_ref, w_ref, mu_ref, lora_ref, w0_ref, a0_ref,
                 kk_ref, ka_ref, rk_ref, gnw_ref, gnb_ref, hsum_ref, tril_ref,
                 y_ref, hn_ref,
                 proj_a, proj_b, carry_ref, state_ref, at_ref, rt_ref, bt_ref, kt_ref, bp_ref, kp_ref,
                 vb_ref, gam_ref, yacc_ref, bonus_ref, *, tt, n_t):
    s = pl.program_id(0)

    @pl.when(s == 0)
    def _():
        proj_b[...] = jnp.zeros_like(proj_b)

    @pl.when(jnp.logical_or(s == 0, lax.rem(s + n_t - 1, n_t) == 0))
    def _():
        carry_ref[...] = jnp.zeros_like(carry_ref)
        state_ref[...] = jnp.zeros_like(state_ref)

    refs = dict(x=x_ref, scale=scale_ref, shift=shift_ref, gain=gain_ref, w=w_ref, mu=mu_ref,
                lora=lora_ref, w0=w0_ref, a0=a0_ref, kk=kk_ref, ka=ka_ref, rk=rk_ref, gnw=gnw_ref,
                gnb=gnb_ref, hsum=hsum_ref, tril=tril_ref, y=y_ref, hn=hn_ref, carry=carry_ref,
                state=state_ref,
                at=at_ref, rt=rt_ref, bt=bt_ref, kt=kt_ref, bp=bp_ref, kp=kp_ref, vb=vb_ref,
                gam=gam_ref, yacc=yacc_ref, bonus=bonus_ref)

    @pl.when(s % 2 == 0)
    def _():
        _rwkv_step(refs, proj_b, proj_a, tt)

    @pl.when(s % 2 == 1)
    def _():
        _rwkv_step(refs, proj_a, proj_b, tt)


def _rwkv_step(R, src_ref, dst_ref, tt):
    L = RWKV_CHUNK
    n_pairs = D_RWKV // LANES
    n_chunks = tt // L

    hn = _adaln_norm(R["x"][0], R["scale"][0], R["shift"][0], R["gain"][...]).astype(BF16)
    R["hn"][0] = hn
    pending = [(o, min(2 * LANES, RWKV_COLS - o)) for o in range(0, RWKV_COLS, 2 * LANES)]

    def tick(n=1):
        for _ in range(n):
            if pending:
                o, w = pending.pop(0)
                dst_ref[:, o:o + w] = _dot(hn, R["w"][:, o:o + w])

    row0 = lax.broadcasted_iota(jnp.int32, (tt, LANES), 0) == 0

    def shifted(g):
        cols = slice(g * LANES, (g + 1) * LANES)
        raw = src_ref[:, cols]
        prev = jnp.where(row0, R["carry"][:, cols], pltpu.roll(raw, 1, 0))
        R["carry"][:, cols] = raw[tt - 1:tt, :]
        return raw + R["mu"][:, cols] * (prev - raw)

    g_lora = shifted(3 * n_pairs)
    lw = _dot(jnp.tanh(g_lora).astype(BF16), R["lora"][0])
    la = _dot(g_lora.astype(BF16), R["lora"][1])
    hsum = R["hsum"][...]
    tick()
    for p in range(n_pairs):
        cols = slice(p * LANES, (p + 1) * LANES)
        r = shifted(p)
        k = shifted(n_pairs + p)
        v = shifted(2 * n_pairs + p)
        ld = -jnp.exp(F32(-0.5)) * _sigmoid(R["w0"][:, cols] + lw[:, cols])
        a = _sigmoid(R["a0"][:, cols] + la[:, cols])
        kk = k * R["kk"][:, cols]
        k2 = k * (1.0 + (a - 1.0) * R["ka"][:, cols])
        sums = _dot(jnp.concatenate([kk * kk, r * k2 * R["rk"][:, cols]], axis=1).astype(BF16), hsum)
        kk = kk / jnp.maximum(jnp.sqrt(sums[:, :LANES]), 1e-12)
        R["bonus"][:, cols] = sums[:, LANES:] * v
        kka = kk * a
        c2 = _dot(R["tril"][...], jnp.concatenate(_split_bf16(ld, 2), axis=1))
        c = c2[:, :LANES] + c2[:, LANES:]
        R["gam"][:, cols] = c
        c_last = jnp.concatenate(
            [jnp.broadcast_to(R["gam"][ci * L + L - 1:ci * L + L, cols], (L, LANES))
             for ci in range(n_chunks)], axis=0)
        gam = jnp.exp(c_last)
        e_pos = jnp.exp(c)
        e_neg = 1.0 / e_pos
        e_rev = gam * e_neg
        R["rt"][:, cols] = (r * e_pos).astype(BF16)
        R["kt"][:, cols] = (k2 * e_neg).astype(BF16)
        R["at"][:, cols] = (-kk * jnp.exp(c - ld)).astype(BF16)
        R["bt"][:, cols] = (kka * e_neg).astype(BF16)
        R["kp"][:, cols] = (k2 * e_rev).astype(BF16)
        R["bp"][:, cols] = (kka * e_rev).astype(BF16)
        R["vb"][:, cols] = v.astype(BF16)
        R["gam"][:, cols] = gam
        tick(2)

    ti = lax.broadcasted_iota(jnp.int32, (L, LANES), 0)
    si = lax.broadcasted_iota(jnp.int32, (L, LANES), 1) % L
    strict = si < ti
    incl = si <= ti
    eye_pair = (si == ti).astype(F32)
    xr = ti ^ si
    r2 = lax.broadcasted_iota(jnp.int32, (LANES, LANES), 0)
    c2 = lax.broadcasted_iota(jnp.int32, (LANES, LANES), 1)
    same_head = (r2 < RWKV_HEAD) == (c2 < RWKV_HEAD)
    eye_full = r2 == c2

    chains = [(ci, p) for p in range(n_pairs) for ci in range(n_chunks)]

    def load(name):
        return [R[name][ci * L:(ci + 1) * L, p * LANES:(p + 1) * LANES] for ci, p in chains]

    at, rt, bt, kt, bp, kp, vb = (load(n) for n in ("at", "rt", "bt", "kt", "bp", "kp", "vb"))
    zero = jnp.zeros((L, LANES), F32)
    zero2 = jnp.zeros((LANES, LANES), F32)
    aa = [_dot_nt(jnp.concatenate([a_, r_], axis=0),
                  jnp.concatenate([_pair_blockdiag(b_), _pair_blockdiag(k_)], axis=0))
          for a_, r_, b_, k_ in zip(at, rt, bt, kt)]
    ab = [jnp.where(strict, x[:L, :LANES], zero) for x in aa]
    ak = [jnp.where(strict, x[:L, LANES:], zero).astype(BF16) for x in aa]
    rb = [jnp.where(incl, x[L:, :LANES], zero).astype(BF16) for x in aa]
    rk = [jnp.where(incl, x[L:, LANES:], zero).astype(BF16) for x in aa]
    tinv = [eye_pair + jnp.where((xr >> 1) == 0, x, zero) for x in ab]
    for lvl in range(1, 6):
        yy = [_dot(jnp.where((xr >> lvl) == 1, x, zero).astype(BF16), _pair_blockdiag(t.astype(BF16)))
              for x, t in zip(ab, tinv)]
        tinv = [t + _dot(t.astype(BF16), _pair_blockdiag(y.astype(BF16))) for t, y in zip(tinv, yy)]
    wv = [_dot(jnp.concatenate([x, z_], axis=0), _pair_blockdiag(y)) for x, z_, y in zip(ak, rk, vb)]
    w = [x[:L].astype(BF16) for x in wv]
    ap = [_dot(t.astype(BF16), jnp.concatenate([_pair_blockdiag(a_), _pair_blockdiag(w_)], axis=1))
          for t, a_, w_ in zip(tinv, at, w)]
    ahat = [x[:, :LANES].astype(BF16) for x in ap]
    p1 = [x[:, LANES:].astype(BF16) for x in ap]
    q = [_dot(x, jnp.concatenate([_pair_blockdiag(a_), _pair_blockdiag(p_)], axis=1))
         for x, a_, p_ in zip(rb, ahat, p1)]
    y_in = [q_[:, LANES:] + x[L:] for q_, x in zip(q, wv)]
    mg = [_dot_tn(jnp.concatenate([b_, k_], axis=0),
                  jnp.concatenate([jnp.concatenate([a_, p_], axis=1),
                                   jnp.concatenate([jnp.zeros_like(v_), v_], axis=1)], axis=0))
          for b_, k_, a_, p_, v_ in zip(bp, kp, ahat, p1, vb)]
    lhs = [jnp.concatenate([r_.astype(F32) + q_[:, :LANES], jnp.where(same_head, m_[:, :LANES], zero2)],
                           axis=0).astype(BF16) for r_, q_, m_ in zip(rt, q, mg)]
    gg = [jnp.where(same_head, m_[:, LANES:], zero2) for m_ in mg]
    gam_col = [jnp.sum(jnp.where(eye_full,
                                 jnp.broadcast_to(R["gam"][ci * L:ci * L + 1, p * LANES:(p + 1) * LANES],
                                                  (LANES, LANES)), zero2), axis=1, keepdims=True)
               for ci, p in chains]

    hs = [R["state"][p] for p in range(n_pairs)]
    for ci in range(n_chunks):
        idx = [p * n_chunks + ci for p in range(n_pairs)]
        yh = [_dot(lhs[i], h.astype(BF16)) for i, h in zip(idx, hs)]
        for p, i in enumerate(idx):
            R["yacc"][ci * L:(ci + 1) * L, p * LANES:(p + 1) * LANES] = yh[p][:L] + y_in[i]
        hs = [gam_col[i] * h + y_[L:] + gg[i] for i, h, y_ in zip(idx, hs, yh)]
    for p in range(n_pairs):
        R["state"][p] = hs[p]

    inv_n = F32(1.0 / RWKV_HEAD)
    for o in range(0, D_RWKV, 2 * LANES):
        cols = slice(o, o + 2 * LANES)
        y = R["yacc"][:, cols]
        mean = _dot(y.astype(BF16), hsum) * inv_n
        d = y - mean
        var = _dot((d * d).astype(BF16), hsum) * inv_n
        yn = d * lax.rsqrt(var + RWKV_GN_EPS) * R["gnw"][:, cols] + R["gnb"][:, cols]
        z = src_ref[:, D_SHIFT + o:D_SHIFT + o + 2 * LANES]
        R["y"][0, :, cols] = ((yn + R["bonus"][:, cols]) * _silu(z)).astype(R["y"].dtype)
        tick(2)
    tick(len(pending))


def _chunk_tri(n, chunk):
    i = jnp.arange(n)[:, None]
    j = jnp.arange(n)[None, :]
    return (((i // chunk) == (j // chunk)) & (j <= i)).astype(BF16)


def _rwkv(h_res, ada, gain, w_all, p, tt, out_dtype):
    bsz, t, d = h_res.shape
    n_t = t // tt
    n_tiles = bsz * n_t
    nxt = lambda s: jnp.minimum(s, n_tiles - 1)
    cur = lambda s: jnp.maximum(s - 1, 0)
    c2 = lambda s: (0, 0)
    vec = lambda n: pl.BlockSpec((1, n), c2)
    hsum = ((jnp.arange(2 * LANES)[:, None] // RWKV_HEAD) ==
            (jnp.arange(2 * LANES)[None, :] // RWKV_HEAD)).astype(BF16)
    lora = jnp.stack([
        jnp.concatenate([p["w_decay_up"], jnp.zeros((LORA, D_RWKV), F32)], axis=0),
        jnp.concatenate([jnp.zeros((LORA, D_RWKV), F32), p["w_icl_up"]], axis=0)]).astype(BF16)
    v1 = lambda a: a.reshape(1, -1).astype(F32)
    act = lambda: pltpu.VMEM((tt, D_RWKV), BF16)
    f32_tile = lambda: pltpu.VMEM((tt, D_RWKV), F32)
    return pl.pallas_call(
        functools.partial(_rwkv_kernel, tt=tt, n_t=n_t),
        grid=(n_tiles + 1,),
        in_specs=[pl.BlockSpec((1, tt, d), lambda s: (nxt(s) // n_t, nxt(s) % n_t, 0)),
                  pl.BlockSpec((1, 1, d), lambda s: (nxt(s) // n_t, 0, 1)),
                  pl.BlockSpec((1, 1, d), lambda s: (nxt(s) // n_t, 0, 0)),
                  vec(d),
                  pl.BlockSpec((d, MLSTM_COLS), c2, pipeline_mode=pl.Buffered(1)),
                  vec(D_SHIFT),
                  pl.BlockSpec((2, 2 * LORA, D_RWKV), lambda s: (0, 0, 0)),
                  vec(D_RWKV), vec(D_RWKV), vec(D_RWKV), vec(D_RWKV), vec(D_RWKV),
                  vec(D_RWKV), vec(D_RWKV),
                  pl.BlockSpec((2 * LANES, 2 * LANES), c2),
                  pl.BlockSpec((tt, tt), c2)],
        out_specs=[pl.BlockSpec((1, tt, D_RWKV), lambda s: (cur(s) // n_t, cur(s) % n_t, 0)),
                   pl.BlockSpec((1, tt, d), lambda s: (nxt(s) // n_t, nxt(s) % n_t, 0))],
        out_shape=[jax.ShapeDtypeStruct((bsz, t, D_RWKV), out_dtype),
                   jax.ShapeDtypeStruct((bsz, t, d), BF16)],
        scratch_shapes=[pltpu.VMEM((tt, RWKV_COLS), F32),
                        pltpu.VMEM((tt, RWKV_COLS), F32),
                        pltpu.VMEM((1, D_SHIFT), F32),
                        pltpu.VMEM((D_RWKV // LANES, LANES, LANES), F32),
                        act(), act(), act(), act(), act(), act(), act(),
                        f32_tile(), f32_tile(), f32_tile()],
        compiler_params=pltpu.CompilerParams(dimension_semantics=("arbitrary",),
                                             vmem_limit_bytes=VMEM_LIMIT),
        name="rwkv7",
    )(h_res, ada.reshape(bsz, 1, 3 * d), ada.reshape(bsz, 1, 3 * d), gain.reshape(1, d), w_all,
      v1(p["mu_rwkv"]), lora, v1(p["w_decay0"]), v1(p["a0"]), v1(p["k_k"]), v1(p["k_a"]),
      v1(p["r_k"]), v1(p["rwkv_gn_w"]), v1(p["rwkv_gn_b"]), hsum, _chunk_tri(tt, RWKV_CHUNK))


M_QK, M_V, M_O, M_Z, M_G = 0, 2 * D_MLSTM, 3 * D_MLSTM, 4 * D_MLSTM, 5 * D_MLSTM


def _mlstm_kernel(hn_ref, w_ref,
                  xo_ref, gate_ref, yr_ref, wout_ref, fg_ref,
                  convw_ref, convb_ref, gbias_ref, lnw_ref, skip_ref, tril_ref,
                  o_ref,
                  proj_a, proj_b, ym_a, ym_b, hres_ref, xbuf_ref, state_ref, m_ref, *, tt, n_t, final):
    s = pl.program_id(0)

    @pl.when(s == 0)
    def _():
        proj_b[...] = jnp.zeros_like(proj_b)
        ym_a[...] = jnp.zeros_like(ym_a)

    @pl.when(jnp.logical_or(s == 0, lax.rem(s + n_t - 1, n_t) == 0))
    def _():
        xbuf_ref[0:SUBLANES, :] = jnp.zeros((SUBLANES, 2 * D_MLSTM), F32)
        state_ref[...] = jnp.zeros_like(state_ref)
        m_ref[...] = jnp.zeros_like(m_ref)

    R = dict(hn=hn_ref, w=w_ref, xo=xo_ref,
             gate=gate_ref, yr=yr_ref, wout=wout_ref, fg=fg_ref, convw=convw_ref, convb=convb_ref,
             gbias=gbias_ref, lnw=lnw_ref, skip=skip_ref, tril=tril_ref, o=o_ref, hres=hres_ref,
             xbuf=xbuf_ref, state=state_ref, m=m_ref)

    @pl.when(s % 2 == 0)
    def _():
        _mlstm_step(R, proj_b, proj_a, ym_b, ym_a, tt, final)

    @pl.when(s % 2 == 1)
    def _():
        _mlstm_step(R, proj_a, proj_b, ym_a, ym_b, tt, final)


def _mlstm_step(R, src_ref, dst_ref, ym_dst, ym_src, tt, final):
    L = MLSTM_CHUNK
    H = MLSTM_HEADS
    dh = MLSTM_HEAD
    d_model = R["xo"].shape[-1]
    wide = 2 * LANES

    y_cat = jnp.concatenate([R["yr"][0], ym_src[...]], axis=1)
    hn = R["hn"][0]

    def out_chunk(o):
        mix = _dot(y_cat, R["wout"][:, o:o + wide])
        R["hres"][:, o:o + wide] = R["xo"][0, :, o:o + wide] + R["gate"][0, :, o:o + wide] * mix

    def proj_chunk(o):
        w = min(wide, MLSTM_COLS - o)
        dst_ref[:, o:o + w] = _dot(hn, R["w"][:, o:o + w])

    pending = ([functools.partial(out_chunk, o) for o in range(0, d_model, wide)] +
               [functools.partial(proj_chunk, o) for o in range(0, MLSTM_COLS, wide)])

    def tick(n=1):
        for _ in range(n):
            if pending:
                pending.pop(0)()

    qk_parts = []
    for o in range(0, 2 * D_MLSTM, wide):
        cols = slice(o, o + wide)
        R["xbuf"][SUBLANES:SUBLANES + tt, cols] = src_ref[:, M_QK + o:M_QK + o + wide]
        xb = R["xbuf"][:, cols]
        acc = R["convb"][:, cols] + R["convw"][CONV_K - 1:CONV_K, cols] * xb[SUBLANES:SUBLANES + tt]
        for j in range(CONV_K - 1):
            sh = CONV_K - 1 - j
            acc = acc + R["convw"][j:j + 1, cols] * pltpu.roll(xb, sh, 0)[SUBLANES:SUBLANES + tt]
        R["xbuf"][0:SUBLANES, cols] = xb[tt:tt + SUBLANES]
        qk_parts.append(_silu(acc))
        tick()
    qk = jnp.concatenate(qk_parts, axis=1)
    q_c = qk[:, :D_MLSTM]
    k_c = qk[:, D_MLSTM:] * F32(dh ** -0.5)

    gt = src_ref[:, M_G:M_G + GATE_PAD] + R["gbias"][...]
    lane = lax.broadcasted_iota(jnp.int32, gt.shape, 1)
    is_f = (lane >= H) & (lane < 2 * H)
    log_f = jnp.minimum(gt, 0.0) - jnp.log(1.0 + jnp.exp(-jnp.abs(gt)))
    gates = jnp.where(is_f, log_f, gt)
    bcum = _dot_split_rhs(R["tril"][...], gates, 2)

    ti = lax.broadcasted_iota(jnp.int32, (L, L), 0)
    si = lax.broadcasted_iota(jnp.int32, (L, L), 1)
    causal = si <= ti
    ones_l = jnp.ones((L, dh), BF16)
    n_chunks = tt // L
    units = [(ci, h) for ci in range(n_chunks) for h in range(H)]
    rows_of = lambda ci: slice(ci * L, (ci + 1) * L)
    cols_of = lambda h: slice(h * dh, (h + 1) * dh)

    g_rows = [jnp.transpose(gates[rows_of(ci)]) for ci in range(n_chunks)]
    b_rows = [jnp.transpose(bcum[rows_of(ci)]) for ci in range(n_chunks)]
    q = [q_c[rows_of(ci), cols_of(h)] for ci, h in units]
    kh = [k_c[rows_of(ci), cols_of(h)] for ci, h in units]
    v_aug = [jnp.concatenate([src_ref[rows_of(ci), M_V + h * dh:M_V + (h + 1) * dh].astype(BF16),
                              ones_l], axis=1) for ci, h in units]
    b_col = [bcum[rows_of(ci), H + h:H + h + 1] for ci, h in units]
    i_col = [gates[rows_of(ci), h:h + 1] for ci, h in units]
    row_v = [g_rows[ci][h:h + 1, :] - b_rows[ci][H + h:H + h + 1, :] for ci, h in units]
    tick()
    d_log = [jnp.where(causal, bc + rv, -jnp.inf) for bc, rv in zip(b_col, row_v)]
    a_t = [jnp.max(d, axis=-1, keepdims=True) for d in d_log]
    tick()
    scores = []
    for q_, k_, d, a in zip(q, kh, d_log, a_t):
        scores.append(_dot_nt(q_.astype(BF16), k_.astype(BF16)) * jnp.exp(d - a))
        if len(scores) % 2 == 0:
            tick()
    b_last = [bc[L - 1:L, :] for bc in b_col]
    a_last = [a[L - 1:L, :] for a in a_t]
    kw = [(k_ * jnp.exp(bl - bc + ic - al)).astype(BF16)
          for k_, bl, bc, ic, al in zip(kh, b_last, b_col, i_col, a_last)]
    upd = [_dot_tn(kw_, va) for kw_, va in zip(kw, v_aug)]
    tick()

    st = [R["state"][h] for h in range(H)]
    m_prev = [R["m"][h:h + 1, 0:1] for h in range(H)]
    for ci in range(n_chunks):
        ids = [ci * H + h for h in range(H)]
        inter_log = [b_col[i] + m_prev[h] for h, i in enumerate(ids)]
        m_t = [jnp.maximum(il, a_t[i]) for il, i in zip(inter_log, ids)]
        nd = [_dot(jnp.concatenate([(jnp.exp(a_t[i] - mt) * scores[i]).astype(BF16),
                                    (jnp.exp(il - mt) * q[i]).astype(BF16)], axis=1),
                   jnp.concatenate([v_aug[i], st[h].astype(BF16)], axis=0))
              for h, (i, mt, il) in enumerate(zip(ids, m_t, inter_log))]
        hh = [x[:, :dh] / jnp.maximum(jnp.abs(x[:, dh:]), jnp.exp(-mt)) for x, mt in zip(nd, m_t)]
        m_new = [mt[L - 1:L, :] for mt in m_t]
        st = [jnp.exp(b_last[i] + mp - mn) * s_ + jnp.exp(a_last[i] - mn) * upd[i]
              for i, mp, mn, s_ in zip(ids, m_prev, m_new, st)]
        m_prev = m_new
        mean = [jnp.mean(x, axis=-1, keepdims=True) for x in hh]
        dv = [x - mu for x, mu in zip(hh, mean)]
        var = [jnp.mean(x * x, axis=-1, keepdims=True) for x in dv]
        for h in range(H):
            rows, cols = rows_of(ci), cols_of(h)
            hn_ = dv[h] * lax.rsqrt(var[h] + MLSTM_LN_EPS) * R["lnw"][:, cols]
            o_gate = _sigmoid(src_ref[rows, M_O + h * dh:M_O + (h + 1) * dh])
            out = o_gate * hn_ + R["skip"][:, cols] * q_c[rows, cols]
            z = src_ref[rows, M_Z + h * dh:M_Z + (h + 1) * dh]
            ym_dst[rows, cols] = (out * _silu(z)).astype(ym_dst.dtype)
            tick()
    for h in range(H):
        R["state"][h] = st[h]
        R["m"][h:h + 1, :] = jnp.broadcast_to(m_prev[h], (1, LANES))
    tick(len(pending))

    hres = R["hres"][...]
    if final:
        ms = jnp.mean(hres * hres, axis=-1, keepdims=True)
        hres = hres * lax.rsqrt(ms + NORM_EPS) * R["fg"][...]
    R["o"][0] = hres.astype(R["o"].dtype)


def _mlstm_out(h_res, hn, w_all, y_r, ada, w_out, final_gain, p, tt, final, out_dtype):
    bsz, t, d = h_res.shape
    n_t = t // tt
    n_tiles = bsz * n_t
    nxt = lambda s: jnp.minimum(s, n_tiles - 1)
    old = lambda s: jnp.clip(s - 2, 0, n_tiles - 1)
    c2 = lambda s: (0, 0)
    vec = lambda n: pl.BlockSpec((1, n), c2)
    v1 = lambda a: a.reshape(1, -1).astype(F32)
    gbias = jnp.concatenate([p["mlstm_b_i"], p["mlstm_b_f"],
                             jnp.zeros((GATE_PAD - 2 * MLSTM_HEADS,), F32)]).reshape(1, GATE_PAD)
    tile_of = lambda f, n: pl.BlockSpec((1, tt, n), lambda s: (f(s) // n_t, f(s) % n_t, 0))
    gate_of = lambda f: pl.BlockSpec((1, 1, d), lambda s: (f(s) // n_t, 0, 2))
    resident = lambda shape: pl.BlockSpec(shape, c2, pipeline_mode=pl.Buffered(1))
    return pl.pallas_call(
        functools.partial(_mlstm_kernel, tt=tt, n_t=n_t, final=final),
        grid=(n_tiles + 2,),
        in_specs=[tile_of(nxt, d),
                  pl.BlockSpec((d, MLSTM_COLS), lambda s: (0, 1), pipeline_mode=pl.Buffered(1)),
                  tile_of(old, d), gate_of(old), tile_of(old, D_RWKV),
                  resident((D_RWKV + D_MLSTM, d)), vec(d),
                  pl.BlockSpec((CONV_K, 2 * D_MLSTM), c2),
                  vec(2 * D_MLSTM), vec(GATE_PAD), vec(D_MLSTM), vec(D_MLSTM),
                  pl.BlockSpec((tt, tt), c2)],
        out_specs=tile_of(old, d),
        out_shape=jax.ShapeDtypeStruct((bsz, t, d), out_dtype),
        scratch_shapes=[pltpu.VMEM((tt, MLSTM_COLS), F32),
                        pltpu.VMEM((tt, MLSTM_COLS), F32),
                        pltpu.VMEM((tt, D_MLSTM), BF16),
                        pltpu.VMEM((tt, D_MLSTM), BF16),
                        pltpu.VMEM((tt, d), F32),
                        pltpu.VMEM((tt + SUBLANES, 2 * D_MLSTM), F32),
                        pltpu.VMEM((MLSTM_HEADS, MLSTM_HEAD, 2 * MLSTM_HEAD), F32),
                        pltpu.VMEM((SUBLANES, LANES), F32)],
        compiler_params=pltpu.CompilerParams(dimension_semantics=("arbitrary",),
                                             vmem_limit_bytes=VMEM_LIMIT),
        name="mlstm_out",
    )(hn, w_all, h_res, ada.reshape(bsz, 1, 3 * d), y_r, w_out, final_gain.reshape(1, d),
      p["mlstm_conv_w"].astype(F32), v1(p["mlstm_conv_b"]), gbias,
      v1(p["mlstm_ln_w"]), v1(p["mlstm_skip"]), _chunk_tri(tt, MLSTM_CHUNK))


def _regroup_w_in(w):
    o = RWKV_COLS + 4 * D_MLSTM
    gates_end = o + 2 * MLSTM_HEADS
    zeros = lambda n: jnp.zeros((w.shape[0], n), w.dtype)
    return jnp.concatenate([w[:, :RWKV_COLS], zeros(MLSTM_COLS - RWKV_COLS),
                            w[:, RWKV_COLS:o], w[:, gates_end:], w[:, o:gates_end],
                            zeros(GATE_PAD - 2 * MLSTM_HEADS)], axis=1).astype(BF16)


def _tile(t, want):
    tile = min(t, want)
    assert t % tile == 0
    return tile


def kernel(x, c, w_ada, b_ada, norm_gain, w_in, mu_rwkv, w_decay_up, w_decay0, w_icl_up, a0, k_k, k_a, r_k, rwkv_gn_w, rwkv_gn_b, mlstm_conv_w, mlstm_conv_b, mlstm_b_i, mlstm_b_f, mlstm_ln_w, mlstm_skip, w_out, final_gain):
    bsz, t, d = x.shape
    depth = w_ada.shape[0]
    tt_r = _tile(t, RWKV_TILE)
    tt_m = _tile(t, MLSTM_TILE)
    assert t % RWKV_CHUNK == 0 and t % MLSTM_CHUNK == 0
    h_res = x.astype(F32)
    c32 = c.astype(F32)
    for l in range(depth):
        ada = _ada(c32, w_ada[l], b_ada[l])
        w_all = _regroup_w_in(w_in[l])
        rp = dict(mu_rwkv=mu_rwkv[l], w_decay_up=w_decay_up[l], w_decay0=w_decay0[l],
                  w_icl_up=w_icl_up[l], a0=a0[l], k_k=k_k[l], k_a=k_a[l], r_k=r_k[l],
                  rwkv_gn_w=rwkv_gn_w[l], rwkv_gn_b=rwkv_gn_b[l])
        y_r, hn = _rwkv(h_res, ada, norm_gain[l], w_all, rp, tt_r, BF16)
        mp = dict(mlstm_conv_w=mlstm_conv_w[l], mlstm_conv_b=mlstm_conv_b[l], mlstm_b_i=mlstm_b_i[l],
                  mlstm_b_f=mlstm_b_f[l], mlstm_ln_w=mlstm_ln_w[l], mlstm_skip=mlstm_skip[l])
        final = l == depth - 1
        h_res = _mlstm_out(h_res, hn, w_all, y_r, ada,
                           w_out[l].astype(BF16), final_gain, mp, tt_m, final,
                           x.dtype if final else F32)
    return h_res
```

```python
import functools

import jax
import jax.numpy as jnp
from jax import lax
from jax.experimental import pallas as pl
from jax.experimental.pallas import tpu as pltpu

F32 = jnp.float32
BF16 = jnp.bfloat16

RWKV_HEAD = 64
RWKV_HEADS = 8
D_RWKV = RWKV_HEAD * RWKV_HEADS
MLSTM_HEADS = 4
MLSTM_HEAD = 128
D_MLSTM = MLSTM_HEADS * MLSTM_HEAD
LORA = 64
D_SHIFT = 3 * D_RWKV + 2 * LORA
CONV_K = 4
NORM_EPS = 1e-6
RWKV_GN_EPS = 64e-5
MLSTM_LN_EPS = 1e-6
GATE_PAD = 128

RWKV_CHUNK = 64
MLSTM_CHUNK = 128
LANES = 128
SUBLANES = 8
RWKV_TILE = 256
MLSTM_TILE = 512
VMEM_LIMIT = 56 * 1024 * 1024


def _dot(a, b):
    return jnp.dot(a, b, preferred_element_type=F32)


def _dot_nt(a, b):
    return lax.dot_general(a, b, (((1,), (1,)), ((), ())), preferred_element_type=F32)


def _dot_tn(a, b):
    return lax.dot_general(a, b, (((0,), (0,)), ((), ())), preferred_element_type=F32)


def _split_bf16(x, n):
    parts = []
    rem = x
    for i in range(n):
        p = rem.astype(BF16)
        parts.append(p)
        if i + 1 < n:
            rem = rem - p.astype(F32)
    return parts


def _dot_split_rhs(m, x, n):
    acc = None
    for p in _split_bf16(x, n):
        d = _dot(m, p)
        acc = d if acc is None else acc + d
    return acc


def _sigmoid(x):
    return 1.0 / (1.0 + jnp.exp(-x))


def _silu(x):
    return x * _sigmoid(x)


def _ada_kernel(c_ref, w_ref, b_ref, o_ref):
    ca = _silu(c_ref[...])
    c_hi, c_lo = _split_bf16(ca, 2)
    w = w_ref[...]
    w_hi, w_lo = _split_bf16(w, 2)
    o_ref[...] = _dot(c_hi, w_hi) + _dot(c_hi, w_lo) + _dot(c_lo, w_hi) + b_ref[...]


def _ada(c, w, b):
    bsz, d = c.shape
    n = w.shape[1]
    bn = d
    return pl.pallas_call(
        _ada_kernel,
        grid=(n // bn,),
        in_specs=[pl.BlockSpec((bsz, d), lambda j: (0, 0)),
                  pl.BlockSpec((d, bn), lambda j: (0, j)),
                  pl.BlockSpec((1, bn), lambda j: (0, j))],
        out_specs=pl.BlockSpec((bsz, bn), lambda j: (0, j)),
        out_shape=jax.ShapeDtypeStruct((bsz, n), F32),
        compiler_params=pltpu.CompilerParams(dimension_semantics=("arbitrary",),
                                             vmem_limit_bytes=VMEM_LIMIT),
        name="ada",
    )(c, w, b.reshape(1, n))


RWKV_COLS = D_SHIFT + D_RWKV
MLSTM_COLS = 5 * D_MLSTM + GATE_PAD
MLSTM_MAIN = 4 * D_MLSTM
MLSTM_TAIL = MLSTM_COLS - MLSTM_MAIN


def _adaln_norm(x, scale, shift, gain):
    ms = jnp.mean(x * x, axis=-1, keepdims=True)
    return x * lax.rsqrt(ms + NORM_EPS) * (gain * (1.0 + scale)) + shift


def _pair_blockdiag(x):
    lane = lax.broadcasted_iota(jnp.int32, x.shape, 1)
    first = lane < RWKV_HEAD
    zero = jnp.zeros_like(x)
    return jnp.concatenate([jnp.where(first, x, zero), jnp.where(first, zero, x)], axis=0)


def _rwkv_kernel(x_ref, scale_ref, shift_ref, gain_ref, w_ref, mu_ref, lora_ref, w0_ref, a0_ref,
                 kk_ref, ka_ref, rk_ref, gnw_ref, gnb_ref, hsum_ref, tril_ref,
                 y_ref, hn_ref,
                 proj_a, proj_b, carry_ref, state_ref, at_ref, rt_ref, bt_ref, kt_ref, bp_ref, kp_ref,
                 vb_ref, gam_ref, yacc_ref, bonus_ref, *, tt, n_t):
    s = pl.program_id(0)

    @pl.when(s == 0)
    def _():
        proj_b[...] = jnp.zeros_like(proj_b)

    @pl.when(jnp.logical_or(s == 0, lax.rem(s + n_t - 1, n_t) == 0))
    def _():
        carry_ref[...] = jnp.zeros_like(carry_ref)
        state_ref[...] = jnp.zeros_like(state_ref)

    refs = dict(x=x_ref, scale=scale_ref, shift=shift_ref, gain=gain_ref, w=w_ref, mu=mu_ref,
                lora=lora_ref, w0=w0_ref, a0=a0_ref, kk=kk_ref, ka=ka_ref, rk=rk_ref, gnw=gnw_ref,
                gnb=gnb_ref, hsum=hsum_ref, tril=tril_ref, y=y_ref, hn=hn_ref, carry=carry_ref,
                state=state_ref,
                at=at_ref, rt=rt_ref, bt=bt_ref, kt=kt_ref, bp=bp_ref, kp=kp_ref, vb=vb_ref,
                gam=gam_ref, yacc=yacc_ref, bonus=bonus_ref)

    @pl.when(s % 2 == 0)
    def _():
        _rwkv_step(refs, proj_b, proj_a, tt)

    @pl.when(s % 2 == 1)
    def _():
        _rwkv_step(refs, proj_a, proj_b, tt)


def _rwkv_step(R, src_ref, dst_ref, tt):
    L = RWKV_CHUNK
    n_pairs = D_RWKV // LANES
    n_chunks = tt // L

    hn = _adaln_norm(R["x"][0], R["scale"][0], R["shift"][0], R["gain"][...]).astype(BF16)
    R["hn"][0] = hn
    pending = [(o, min(2 * LANES, RWKV_COLS - o)) for o in range(0, RWKV_COLS, 2 * LANES)]

    def tick(n=1):
        for _ in range(n):
            if pending:
                o, w = pending.pop(0)
                dst_ref[:, o:o + w] = _dot(hn, R["w"][:, o:o + w])

    row0 = lax.broadcasted_iota(jnp.int32, (tt, LANES), 0) == 0

    def shifted(g):
        cols = slice(g * LANES, (g + 1) * LANES)
        raw = src_ref[:, cols]
        prev = jnp.where(row0, R["carry"][:, cols], pltpu.roll(raw, 1, 0))
        R["carry"][:, cols] = raw[tt - 1:tt, :]
        return raw + R["mu"][:, cols] * (prev - raw)

    g_lora = shifted(3 * n_pairs)
    lw = _dot(jnp.tanh(g_lora).astype(BF16), R["lora"][0])
    la = _dot(g_lora.astype(BF16), R["lora"][1])
    hsum = R["hsum"][...]
    tick()
    for p in range(n_pairs):
        cols = slice(p * LANES, (p + 1) * LANES)
        r = shifted(p)
        k = shifted(n_pairs + p)
        v = shifted(2 * n_pairs + p)
        ld = -jnp.exp(F32(-0.5)) * _sigmoid(R["w0"][:, cols] + lw[:, cols])
        a = _sigmoid(R["a0"][:, cols] + la[:, cols])
        kk = k * R["kk"][:, cols]
        k2 = k * (1.0 + (a - 1.0) * R["ka"][:, cols])
        sums = _dot(jnp.concatenate([kk * kk, r * k2 * R["rk"][:, cols]], axis=1).astype(BF16), hsum)
        kk = kk / jnp.maximum(jnp.sqrt(sums[:, :LANES]), 1e-12)
        R["bonus"][:, cols] = sums[:, LANES:] * v
        kka = kk * a
        c2 = _dot(R["tril"][...], jnp.concatenate(_split_bf16(ld, 2), axis=1))
        c = c2[:, :LANES] + c2[:, LANES:]
        R["gam"][:, cols] = c
        c_last = jnp.concatenate(
            [jnp.broadcast_to(R["gam"][ci * L + L - 1:ci * L + L, cols], (L, LANES))
             for ci in range(n_chunks)], axis=0)
        gam = jnp.exp(c_last)
        e_pos = jnp.exp(c)
        e_neg = 1.0 / e_pos
        e_rev = gam * e_neg
        R["rt"][:, cols] = (r * e_pos).astype(BF16)
        R["kt"][:, cols] = (k2 * e_neg).astype(BF16)
        R["at"][:, cols] = (-kk * jnp.exp(c - ld)).astype(BF16)
        R["bt"][:, cols] = (kka * e_neg).astype(BF16)
        R["kp"][:, cols] = (k2 * e_rev).astype(BF16)
        R["bp"][:, cols] = (kka * e_rev).astype(BF16)
        R["vb"][:, cols] = v.astype(BF16)
        R["gam"][:, cols] = gam
        tick(2)

    ti = lax.broadcasted_iota(jnp.int32, (L, LANES), 0)
    si = lax.broadcasted_iota(jnp.int32, (L, LANES), 1) % L
    strict = si < ti
    incl = si <= ti
    eye_pair = (si == ti).astype(F32)
    xr = ti ^ si
    r2 = lax.broadcasted_iota(jnp.int32, (LANES, LANES), 0)
    c2 = lax.broadcasted_iota(jnp.int32, (LANES, LANES), 1)
    same_head = (r2 < RWKV_HEAD) == (c2 < RWKV_HEAD)
    eye_full = r2 == c2

    chains = [(ci, p) for p in range(n_pairs) for ci in range(n_chunks)]

    def load(name):
        return [R[name][ci * L:(ci + 1) * L, p * LANES:(p + 1) * LANES] for ci, p in chains]

    at, rt, bt, kt, bp, kp, vb = (load(n) for n in ("at", "rt", "bt", "kt", "bp", "kp", "vb"))
    zero = jnp.zeros((L, LANES), F32)
    zero2 = jnp.zeros((LANES, LANES), F32)
    aa = [_dot_nt(jnp.concatenate([a_, r_], axis=0),
                  jnp.concatenate([_pair_blockdiag(b_), _pair_blockdiag(k_)], axis=0))
          for a_, r_, b_, k_ in zip(at, rt, bt, kt)]
    ab = [jnp.where(strict, x[:L, :LANES], zero) for x in aa]
    ak = [jnp.where(strict, x[:L, LANES:], zero).astype(BF16) for x in aa]
    rb = [jnp.where(incl, x[L:, :LANES], zero).astype(BF16) for x in aa]
    rk = [jnp.where(incl, x[L:, LANES:], zero).astype(BF16) for x in aa]
    tinv = [eye_pair + jnp.where((xr >> 1) == 0, x, zero) for x in ab]
    for lvl in range(1, 6):
        yy = [_dot(jnp.where((xr >> lvl) == 1, x, zero).astype(BF16), _pair_blockdiag(t.astype(BF16)))
              for x, t in zip(ab, tinv)]
        tinv = [t + _dot(t.astype(BF16), _pair_blockdiag(y.astype(BF16))) for t, y in zip(tinv, yy)]
    wv = [_dot(jnp.concatenate([x, z_], axis=0), _pair_blockdiag(y)) for x, z_, y in zip(ak, rk, vb)]
    w = [x[:L].astype(BF16) for x in wv]
    ap = [_dot(t.astype(BF16), jnp.concatenate([_pair_blockdiag(a_), _pair_blockdiag(w_)], axis=1))
          for t, a_, w_ in zip(tinv, at, w)]
    ahat = [x[:, :LANES].astype(BF16) for x in ap]
    p1 = [x[:, LANES:].astype(BF16) for x in ap]
    q = [_dot(x, jnp.concatenate([_pair_blockdiag(a_), _pair_blockdiag(p_)], axis=1))
         for x, a_, p_ in zip(rb, ahat, p1)]
    y_in = [q_[:, LANES:] + x[L:] for q_, x in zip(q, wv)]
    mg = [_dot_tn(jnp.concatenate([b_, k_], axis=0),
                  jnp.concatenate([jnp.concatenate([a_, p_], axis=1),
                                   jnp.concatenate([jnp.zeros_like(v_), v_], axis=1)], axis=0))
          for b_, k_, a_, p_, v_ in zip(bp, kp, ahat, p1, vb)]
    lhs = [jnp.concatenate([r_.astype(F32) + q_[:, :LANES], jnp.where(same_head, m_[:, :LANES], zero2)],
                           axis=0).astype(BF16) for r_, q_, m_ in zip(rt, q, mg)]
    gg = [jnp.where(same_head, m_[:, LANES:], zero2) for m_ in mg]
    gam_col = [jnp.sum(jnp.where(eye_full,
                                 jnp.broadcast_to(R["gam"][ci * L:ci * L + 1, p * LANES:(p + 1) * LANES],
                                                  (LANES, LANES)), zero2), axis=1, keepdims=True)
               for ci, p in chains]

    hs = [R["state"][p] for p in range(n_pairs)]
    for ci in range(n_chunks):
        idx = [p * n_chunks + ci for p in range(n_pairs)]
        yh = [_dot(lhs[i], h.astype(BF16)) for i, h in zip(idx, hs)]
        for p, i in enumerate(idx):
            R["yacc"][ci * L:(ci + 1) * L, p * LANES:(p + 1) * LANES] = yh[p][:L] + y_in[i]
        hs = [gam_col[i] * h + y_[L:] + gg[i] for i, h, y_ in zip(idx, hs, yh)]
    for p in range(n_pairs):
        R["state"][p] = hs[p]

    inv_n = F32(1.0 / RWKV_HEAD)
    for o in range(0, D_RWKV, 2 * LANES):
        cols = slice(o, o + 2 * LANES)
        y = R["yacc"][:, cols]
        mean = _dot(y.astype(BF16), hsum) * inv_n
        d = y - mean
        var = _dot((d * d).astype(BF16), hsum) * inv_n
        yn = d * lax.rsqrt(var + RWKV_GN_EPS) * R["gnw"][:, cols] + R["gnb"][:, cols]
        z = src_ref[:, D_SHIFT + o:D_SHIFT + o + 2 * LANES]
        R["y"][0, :, cols] = ((yn + R["bonus"][:, cols]) * _silu(z)).astype(R["y"].dtype)
        tick(2)
    tick(len(pending))


def _chunk_tri(n, chunk):
    i = jnp.arange(n)[:, None]
    j = jnp.arange(n)[None, :]
    return (((i // chunk) == (j // chunk)) & (j <= i)).astype(BF16)


def _rwkv(h_res, ada, gain, w_bf16, p, tt, out_dtype):
    bsz, t, d = h_res.shape
    n_t = t // tt
    n_tiles = bsz * n_t
    nxt = lambda s: jnp.minimum(s, n_tiles - 1)
    cur = lambda s: jnp.maximum(s - 1, 0)
    c2 = lambda s: (0, 0)
    vec = lambda n: pl.BlockSpec((1, n), c2)
    hsum = ((jnp.arange(2 * LANES)[:, None] // RWKV_HEAD) ==
            (jnp.arange(2 * LANES)[None, :] // RWKV_HEAD)).astype(BF16)
    lora = jnp.stack([
        jnp.concatenate([p["w_decay_up"], jnp.zeros((LORA, D_RWKV), F32)], axis=0),
        jnp.concatenate([jnp.zeros((LORA, D_RWKV), F32), p["w_icl_up"]], axis=0)]).astype(BF16)
    v1 = lambda a: a.reshape(1, -1).astype(F32)
    act = lambda: pltpu.VMEM((tt, D_RWKV), BF16)
    f32_tile = lambda: pltpu.VMEM((tt, D_RWKV), F32)
    return pl.pallas_call(
        functools.partial(_rwkv_kernel, tt=tt, n_t=n_t),
        grid=(n_tiles + 1,),
        in_specs=[pl.BlockSpec((1, tt, d), lambda s: (nxt(s) // n_t, nxt(s) % n_t, 0)),
                  pl.BlockSpec((1, 1, d), lambda s: (nxt(s) // n_t, 0, 1)),
                  pl.BlockSpec((1, 1, d), lambda s: (nxt(s) // n_t, 0, 0)),
                  vec(d),
                  pl.BlockSpec((d, RWKV_COLS), c2, pipeline_mode=pl.Buffered(1)),
                  vec(D_SHIFT),
                  pl.BlockSpec((2, 2 * LORA, D_RWKV), lambda s: (0, 0, 0)),
                  vec(D_RWKV), vec(D_RWKV), vec(D_RWKV), vec(D_RWKV), vec(D_RWKV),
                  vec(D_RWKV), vec(D_RWKV),
                  pl.BlockSpec((2 * LANES, 2 * LANES), c2),
                  pl.BlockSpec((tt, tt), c2)],
        out_specs=[pl.BlockSpec((1, tt, D_RWKV), lambda s: (cur(s) // n_t, cur(s) % n_t, 0)),
                   pl.BlockSpec((1, tt, d), lambda s: (nxt(s) // n_t, nxt(s) % n_t, 0))],
        out_shape=[jax.ShapeDtypeStruct((bsz, t, D_RWKV), out_dtype),
                   jax.ShapeDtypeStruct((bsz, t, d), BF16)],
        scratch_shapes=[pltpu.VMEM((tt, RWKV_COLS), F32),
                        pltpu.VMEM((tt, RWKV_COLS), F32),
                        pltpu.VMEM((1, D_SHIFT), F32),
                        pltpu.VMEM((D_RWKV // LANES, LANES, LANES), F32),
                        act(), act(), act(), act(), act(), act(), act(),
                        f32_tile(), f32_tile(), f32_tile()],
        compiler_params=pltpu.CompilerParams(dimension_semantics=("arbitrary",),
                                             vmem_limit_bytes=VMEM_LIMIT),
        name="rwkv7",
    )(h_res, ada.reshape(bsz, 1, 3 * d), ada.reshape(bsz, 1, 3 * d), gain.reshape(1, d), w_bf16,
      v1(p["mu_rwkv"]), lora, v1(p["w_decay0"]), v1(p["a0"]), v1(p["k_k"]), v1(p["k_a"]),
      v1(p["r_k"]), v1(p["rwkv_gn_w"]), v1(p["rwkv_gn_b"]), hsum, _chunk_tri(tt, RWKV_CHUNK))


M_QK, M_V, M_O, M_Z, M_G = 0, 2 * D_MLSTM, 3 * D_MLSTM, 4 * D_MLSTM, 5 * D_MLSTM


def _mlstm_kernel(hn_ref, w_ref, wt_ref,
                  xo_ref, gate_ref, yr_ref, wout_ref, fg_ref,
                  convw_ref, convb_ref, gbias_ref, lnw_ref, skip_ref, tril_ref,
                  o_ref,
                  proj_a, proj_b, ym_a, ym_b, hres_ref, xbuf_ref, state_ref, m_ref, *, tt, n_t, final):
    s = pl.program_id(0)

    @pl.when(s == 0)
    def _():
        proj_b[...] = jnp.zeros_like(proj_b)
        ym_a[...] = jnp.zeros_like(ym_a)

    @pl.when(jnp.logical_or(s == 0, lax.rem(s + n_t - 1, n_t) == 0))
    def _():
        xbuf_ref[0:SUBLANES, :] = jnp.zeros((SUBLANES, 2 * D_MLSTM), F32)
        state_ref[...] = jnp.zeros_like(state_ref)
        m_ref[...] = jnp.zeros_like(m_ref)

    R = dict(hn=hn_ref, w=w_ref, wt=wt_ref, xo=xo_ref,
             gate=gate_ref, yr=yr_ref, wout=wout_ref, fg=fg_ref, convw=convw_ref, convb=convb_ref,
             gbias=gbias_ref, lnw=lnw_ref, skip=skip_ref, tril=tril_ref, o=o_ref, hres=hres_ref,
             xbuf=xbuf_ref, state=state_ref, m=m_ref)

    @pl.when(s % 2 == 0)
    def _():
        _mlstm_step(R, proj_b, proj_a, ym_b, ym_a, tt, final)

    @pl.when(s % 2 == 1)
    def _():
        _mlstm_step(R, proj_a, proj_b, ym_a, ym_b, tt, final)


def _mlstm_step(R, src_ref, dst_ref, ym_dst, ym_src, tt, final):
    L = MLSTM_CHUNK
    H = MLSTM_HEADS
    dh = MLSTM_HEAD
    d_model = R["xo"].shape[-1]
    wide = 2 * LANES

    y_cat = jnp.concatenate([R["yr"][0], ym_src[...]], axis=1)
    hn = R["hn"][0]

    def out_chunk(o):
        mix = _dot(y_cat, R["wout"][:, o:o + wide])
        R["hres"][:, o:o + wide] = R["xo"][0, :, o:o + wide] + R["gate"][0, :, o:o + wide] * mix

    def proj_chunk(o):
        w = min(wide, MLSTM_COLS - o)
        if o < MLSTM_MAIN:
            dst_ref[:, o:o + w] = _dot(hn, R["w"][:, o:o + w])
        else:
            dst_ref[:, o:o + w] = _dot(hn, R["wt"][:, o - MLSTM_MAIN:o - MLSTM_MAIN + w])

    pending = ([functools.partial(out_chunk, o) for o in range(0, d_model, wide)] +
               [functools.partial(proj_chunk, o) for o in range(0, MLSTM_COLS, wide)])

    def tick(n=1):
        for _ in range(n):
            if pending:
                pending.pop(0)()

    qk_parts = []
    for o in range(0, 2 * D_MLSTM, wide):
        cols = slice(o, o + wide)
        R["xbuf"][SUBLANES:SUBLANES + tt, cols] = src_ref[:, M_QK + o:M_QK + o + wide]
        xb = R["xbuf"][:, cols]
        acc = R["convb"][:, cols] + R["convw"][CONV_K - 1:CONV_K, cols] * xb[SUBLANES:SUBLANES + tt]
        for j in range(CONV_K - 1):
            sh = CONV_K - 1 - j
            acc = acc + R["convw"][j:j + 1, cols] * pltpu.roll(xb, sh, 0)[SUBLANES:SUBLANES + tt]
        R["xbuf"][0:SUBLANES, cols] = xb[tt:tt + SUBLANES]
        qk_parts.append(_silu(acc))
        tick()
    qk = jnp.concatenate(qk_parts, axis=1)
    q_c = qk[:, :D_MLSTM]
    k_c = qk[:, D_MLSTM:] * F32(dh ** -0.5)

    gt = src_ref[:, M_G:M_G + GATE_PAD] + R["gbias"][...]
    lane = lax.broadcasted_iota(jnp.int32, gt.shape, 1)
    is_f = (lane >= H) & (lane < 2 * H)
    log_f = jnp.minimum(gt, 0.0) - jnp.log(1.0 + jnp.exp(-jnp.abs(gt)))
    gates = jnp.where(is_f, log_f, gt)
    bcum = _dot_split_rhs(R["tril"][...], gates, 2)

    ti = lax.broadcasted_iota(jnp.int32, (L, L), 0)
    si = lax.broadcasted_iota(jnp.int32, (L, L), 1)
    causal = si <= ti
    ones_l = jnp.ones((L, dh), BF16)
    n_chunks = tt // L
    units = [(ci, h) for ci in range(n_chunks) for h in range(H)]
    rows_of = lambda ci: slice(ci * L, (ci + 1) * L)
    cols_of = lambda h: slice(h * dh, (h + 1) * dh)

    g_rows = [jnp.transpose(gates[rows_of(ci)]) for ci in range(n_chunks)]
    b_rows = [jnp.transpose(bcum[rows_of(ci)]) for ci in range(n_chunks)]
    q = [q_c[rows_of(ci), cols_of(h)] for ci, h in units]
    kh = [k_c[rows_of(ci), cols_of(h)] for ci, h in units]
    v_aug = [jnp.concatenate([src_ref[rows_of(ci), M_V + h * dh:M_V + (h + 1) * dh].astype(BF16),
                              ones_l], axis=1) for ci, h in units]
    b_col = [bcum[rows_of(ci), H + h:H + h + 1] for ci, h in units]
    i_col = [gates[rows_of(ci), h:h + 1] for ci, h in units]
    row_v = [g_rows[ci][h:h + 1, :] - b_rows[ci][H + h:H + h + 1, :] for ci, h in units]
    tick()
    d_log = [jnp.where(causal, bc + rv, -jnp.inf) for bc, rv in zip(b_col, row_v)]
    a_t = [jnp.max(d, axis=-1, keepdims=True) for d in d_log]
    tick()
    scores = []
    for q_, k_, d, a in zip(q, kh, d_log, a_t):
        scores.append(_dot_nt(q_.astype(BF16), k_.astype(BF16)) * jnp.exp(d - a))
        if len(scores) % 2 == 0:
            tick()
    b_last = [bc[L - 1:L, :] for bc in b_col]
    a_last = [a[L - 1:L, :] for a in a_t]
    kw = [(k_ * jnp.exp(bl - bc + ic - al)).astype(BF16)
          for k_, bl, bc, ic, al in zip(kh, b_last, b_col, i_col, a_last)]
    upd = [_dot_tn(kw_, va) for kw_, va in zip(kw, v_aug)]
    tick()

    st = [R["state"][h] for h in range(H)]
    m_prev = [R["m"][h:h + 1, 0:1] for h in range(H)]
    for ci in range(n_chunks):
        ids = [ci * H + h for h in range(H)]
        inter_log = [b_col[i] + m_prev[h] for h, i in enumerate(ids)]
        m_t = [jnp.maximum(il, a_t[i]) for il, i in zip(inter_log, ids)]
        nd = [_dot(jnp.concatenate([(jnp.exp(a_t[i] - mt) * scores[i]).astype(BF16),
                                    (jnp.exp(il - mt) * q[i]).astype(BF16)], axis=1),
                   jnp.concatenate([v_aug[i], st[h].astype(BF16)], axis=0))
              for h, (i, mt, il) in enumerate(zip(ids, m_t, inter_log))]
        hh = [x[:, :dh] / jnp.maximum(jnp.abs(x[:, dh:]), jnp.exp(-mt)) for x, mt in zip(nd, m_t)]
        m_new = [mt[L - 1:L, :] for mt in m_t]
        st = [jnp.exp(b_last[i] + mp - mn) * s_ + jnp.exp(a_last[i] - mn) * upd[i]
              for i, mp, mn, s_ in zip(ids, m_prev, m_new, st)]
        m_prev = m_new
        mean = [jnp.mean(x, axis=-1, keepdims=True) for x in hh]
        dv = [x - mu for x, mu in zip(hh, mean)]
        var = [jnp.mean(x * x, axis=-1, keepdims=True) for x in dv]
        for h in range(H):
            rows, cols = rows_of(ci), cols_of(h)
            hn_ = dv[h] * lax.rsqrt(var[h] + MLSTM_LN_EPS) * R["lnw"][:, cols]
            o_gate = _sigmoid(src_ref[rows, M_O + h * dh:M_O + (h + 1) * dh])
            out = o_gate * hn_ + R["skip"][:, cols] * q_c[rows, cols]
            z = src_ref[rows, M_Z + h * dh:M_Z + (h + 1) * dh]
            ym_dst[rows, cols] = (out * _silu(z)).astype(ym_dst.dtype)
            tick()
    for h in range(H):
        R["state"][h] = st[h]
        R["m"][h:h + 1, :] = jnp.broadcast_to(m_prev[h], (1, LANES))
    tick(len(pending))

    hres = R["hres"][...]
    if final:
        ms = jnp.mean(hres * hres, axis=-1, keepdims=True)
        hres = hres * lax.rsqrt(ms + NORM_EPS) * R["fg"][...]
    R["o"][0] = hres.astype(R["o"].dtype)


def _mlstm_out(h_res, hn, w_bf16, w_tail, y_r, ada, w_out, final_gain, p, tt, final, out_dtype):
    bsz, t, d = h_res.shape
    n_t = t // tt
    n_tiles = bsz * n_t
    nxt = lambda s: jnp.minimum(s, n_tiles - 1)
    old = lambda s: jnp.clip(s - 2, 0, n_tiles - 1)
    c2 = lambda s: (0, 0)
    vec = lambda n: pl.BlockSpec((1, n), c2)
    v1 = lambda a: a.reshape(1, -1).astype(F32)
    gbias = jnp.concatenate([p["mlstm_b_i"], p["mlstm_b_f"],
                             jnp.zeros((GATE_PAD - 2 * MLSTM_HEADS,), F32)]).reshape(1, GATE_PAD)
    tile_of = lambda f, n: pl.BlockSpec((1, tt, n), lambda s: (f(s) // n_t, f(s) % n_t, 0))
    gate_of = lambda f: pl.BlockSpec((1, 1, d), lambda s: (f(s) // n_t, 0, 2))
    resident = lambda shape: pl.BlockSpec(shape, c2, pipeline_mode=pl.Buffered(1))
    return pl.pallas_call(
        functools.partial(_mlstm_kernel, tt=tt, n_t=n_t, final=final),
        grid=(n_tiles + 2,),
        in_specs=[tile_of(nxt, d),
                  pl.BlockSpec((pl.Element(d), pl.Element(MLSTM_MAIN)), lambda s: (0, RWKV_COLS),
                               pipeline_mode=pl.Buffered(1)),
                  resident((d, MLSTM_TAIL)),
                  tile_of(old, d), gate_of(old), tile_of(old, D_RWKV),
                  resident((D_RWKV + D_MLSTM, d)), vec(d),
                  pl.BlockSpec((CONV_K, 2 * D_MLSTM), c2),
                  vec(2 * D_MLSTM), vec(GATE_PAD), vec(D_MLSTM), vec(D_MLSTM),
                  pl.BlockSpec((tt, tt), c2)],
        out_specs=tile_of(old, d),
        out_shape=jax.ShapeDtypeStruct((bsz, t, d), out_dtype),
        scratch_shapes=[pltpu.VMEM((tt, MLSTM_COLS), F32),
                        pltpu.VMEM((tt, MLSTM_COLS), F32),
                        pltpu.VMEM((tt, D_MLSTM), BF16),
                        pltpu.VMEM((tt, D_MLSTM), BF16),
                        pltpu.VMEM((tt, d), F32),
                        pltpu.VMEM((tt + SUBLANES, 2 * D_MLSTM), F32),
                        pltpu.VMEM((MLSTM_HEADS, MLSTM_HEAD, 2 * MLSTM_HEAD), F32),
                        pltpu.VMEM((SUBLANES, LANES), F32)],
        compiler_params=pltpu.CompilerParams(dimension_semantics=("arbitrary",),
                                             vmem_limit_bytes=VMEM_LIMIT),
        name="mlstm_out",
    )(hn, w_bf16, w_tail, h_res, ada.reshape(bsz, 1, 3 * d), y_r, w_out, final_gain.reshape(1, d),
      p["mlstm_conv_w"].astype(F32), v1(p["mlstm_conv_b"]), gbias,
      v1(p["mlstm_ln_w"]), v1(p["mlstm_skip"]), _chunk_tri(tt, MLSTM_CHUNK))


def _cast_w_in(w):
    gates = RWKV_COLS + MLSTM_MAIN
    gates_end = gates + 2 * MLSTM_HEADS
    pad = jnp.zeros((w.shape[0], GATE_PAD - 2 * MLSTM_HEADS), w.dtype)
    w_tail = jnp.concatenate([w[:, gates_end:], w[:, gates:gates_end], pad], axis=1)
    return w.astype(BF16), w_tail.astype(BF16)


def _tile(t, want):
    tile = min(t, want)
    assert t % tile == 0
    return tile


def kernel(x, c, w_ada, b_ada, norm_gain, w_in, mu_rwkv, w_decay_up, w_decay0, w_icl_up, a0, k_k, k_a, r_k, rwkv_gn_w, rwkv_gn_b, mlstm_conv_w, mlstm_conv_b, mlstm_b_i, mlstm_b_f, mlstm_ln_w, mlstm_skip, w_out, final_gain):
    bsz, t, d = x.shape
    depth = w_ada.shape[0]
    tt_r = _tile(t, RWKV_TILE)
    tt_m = _tile(t, MLSTM_TILE)
    assert t % RWKV_CHUNK == 0 and t % MLSTM_CHUNK == 0
    h_res = x.astype(F32)
    c32 = c.astype(F32)
    for l in range(depth):
        ada = _ada(c32, w_ada[l], b_ada[l])
        w_bf16, w_tail = _cast_w_in(w_in[l])
        rp = dict(mu_rwkv=mu_rwkv[l], w_decay_up=w_decay_up[l], w_decay0=w_decay0[l],
                  w_icl_up=w_icl_up[l], a0=a0[l], k_k=k_k[l], k_a=k_a[l], r_k=r_k[l],
                  rwkv_gn_w=rwkv_gn_w[l], rwkv_gn_b=rwkv_gn_b[l])
        y_r, hn = _rwkv(h_res, ada, norm_gain[l], w_bf16, rp, tt_r, BF16)
        mp = dict(mlstm_conv_w=mlstm_conv_w[l], mlstm_conv_b=mlstm_conv_b[l], mlstm_b_i=mlstm_b_i[l],
                  mlstm_b_f=mlstm_b_f[l], mlstm_ln_w=mlstm_ln_w[l], mlstm_skip=mlstm_skip[l])
        final = l == depth - 1
        h_res = _mlstm_out(h_res, hn, w_bf16, w_tail, y_r, ada,
                           w_out[l].astype(BF16), final_gain, mp, tt_m, final,
                           x.dtype if final else F32)
    return h_res
```
